```python
import jax, jax.numpy as jnp
from jax import lax
import numpy as np

D_MODEL = 1024
BATCH = 4
SEQ = 4096
DEPTH = 2
DEC_BATCH = 2
DEC_SEQ = 16384
PAST_LEN = 128

MIX_WIDTH = 768
N_MEM = 256
X_HEADS = 4
X_HEAD_DIM = 64
X_WIDTH = X_HEADS * X_HEAD_DIM
HG_EXPAND = 128
HG_HEADS = MIX_WIDTH // HG_EXPAND
HG_DV = MIX_WIDTH // HG_HEADS
HG_CHUNK = 64
HEAD_DIM = 64
N_Q_HEADS = MIX_WIDTH // HEAD_DIM
N_KV_HEADS = 4
GROUP = N_Q_HEADS // N_KV_HEADS
KV_WIDTH = N_KV_HEADS * HEAD_DIM
WINDOW = 128
BLOCK = 128
ROPE_THETA = 10000.0
D_FF = 2816
CONV_W = 3
EPS = 1e-6
N_A = (DEPTH + 1) // 2
N_B = DEPTH // 2
HG_IN = 5 * MIX_WIDTH + X_WIDTH
GQA_IN = MIX_WIDTH + 2 * KV_WIDTH + X_WIDTH
OUT_IN = MIX_WIDTH + X_WIDTH

kernel_name = "hybrid_hgrn2_swa_mem_encoder"

F32 = jnp.float32


def rmsnorm(x, w):
    xf = x.astype(F32)
    y = xf * lax.rsqrt(jnp.mean(xf * xf, axis=-1, keepdims=True) + EPS)
    return (y * w.astype(F32)).astype(x.dtype)


def rope(x, pos):
    hd = x.shape[-1]
    freqs = ROPE_THETA ** (-jnp.arange(0, hd, 2, dtype=F32) / hd)
    ang = pos[:, None] * freqs[None, :]
    cos = jnp.cos(ang)[None, :, None, :]
    sin = jnp.sin(ang)[None, :, None, :]
    xf = x.astype(F32)
    x1, x2 = xf[..., : hd // 2], xf[..., hd // 2:]
    return jnp.concatenate([x1 * cos - x2 * sin, x2 * cos + x1 * sin], axis=-1).astype(x.dtype)


def gla_chunked(q, k, v, g):
    B, S, H, dk = q.shape
    dv = v.shape[-1]
    C = HG_CHUNK
    N = S // C
    q, k, g = (t.reshape(B, N, C, H, dk) for t in (q, k, g))
    v = v.reshape(B, N, C, H, dv)
    b = jnp.cumsum(g, axis=2)
    b_last = b[:, :, -1:]
    q_t = q * jnp.exp(b)
    k_t = k * jnp.exp(-b)
    k_end = k * jnp.exp(b_last - b)
    scores = jnp.einsum('bnchd,bnshd->bnhcs', q_t, k_t)
    causal_in_chunk = jnp.tril(jnp.ones((C, C), dtype=bool))
    scores = jnp.where(causal_in_chunk, scores, 0.0)
    o_intra = jnp.einsum('bnhcs,bnshv->bnchv', scores, v)
    decay = jnp.exp(b_last[:, :, 0])

    def step(state, xs):
        qn, kn, vn, dn = xs
        o = jnp.einsum('bchd,bhdv->bchv', qn, state)
        state = dn[..., None] * state + jnp.einsum('bchd,bchv->bhdv', kn, vn)
        return state, o

    xs = (jnp.moveaxis(q_t, 1, 0), jnp.moveaxis(k_end, 1, 0),
          jnp.moveaxis(v, 1, 0), jnp.moveaxis(decay, 1, 0))
    s0 = jnp.zeros((B, H, dk, dv), F32)
    _, o_inter = lax.scan(step, s0, xs)
    o = o_intra + jnp.moveaxis(o_inter, 0, 1)
    return o.reshape(B, S, H, dv)


def hgrn2_mixer(cols, lb_fwd, lb_bwd, gn_w):
    B, S, _ = cols.shape
    q_raw, f_fw, f_bw, i_in, g_out = jnp.split(cols, 5, axis=-1)
    heads = lambda t: t.reshape(B, S, HG_HEADS, -1)
    q = heads(jax.nn.silu(q_raw.astype(F32)))
    v = heads(i_in.astype(F32))

    def gates(f_raw, lb):
        lb = lb.astype(F32)
        f = lb + (1.0 - lb) * jax.nn.sigmoid(f_raw.astype(F32))
        return heads(1.0 - f), heads(jnp.log(f))

    k_f, g_f = gates(f_fw, lb_fwd)
    k_b, g_b = gates(f_bw, lb_bwd)
    o_f = gla_chunked(q, k_f, v, g_f)
    rev = lambda t: jnp.flip(t, axis=1)
    o_b = rev(gla_chunked(rev(q), rev(k_b), rev(v), rev(g_b)))
    o = (o_f + o_b).reshape(B, S, MIX_WIDTH)
    o = rmsnorm(o, gn_w) * jax.nn.silu(g_out.astype(F32))
    return o.astype(cols.dtype)


def window_attn(q, k, v, sink):
    B, S, Hkv, G, hd = q.shape
    nb = S // BLOCK
    pad = ((0, 0), (BLOCK, BLOCK), (0, 0), (0, 0))
    kp = jnp.pad(k.astype(F32), pad).reshape(B, nb + 2, BLOCK, Hkv, hd)
    vp = jnp.pad(v.astype(F32), pad).reshape(B, nb + 2, BLOCK, Hkv, hd)
    kw = jnp.concatenate([kp[:, :-2], kp[:, 1:-1], kp[:, 2:]], axis=2)
    vw = jnp.concatenate([vp[:, :-2], vp[:, 1:-1], vp[:, 2:]], axis=2)
    qb = q.astype(F32).reshape(B, nb, BLOCK, Hkv, G, hd)
    s = jnp.einsum('bnqhgd,bnkhd->bnhgqk', qb, kw) * (hd ** -0.5)
    qi = jnp.arange(BLOCK)[:, None]
    kj = jnp.arange(3 * BLOCK)[None, :]
    band = jnp.abs(kj - BLOCK - qi) <= WINDOW
    kpos = jnp.arange(nb)[:, None] * BLOCK - BLOCK + jnp.arange(3 * BLOCK)[None, :]
    in_seq = (kpos >= 0) & (kpos < S)
    mask = band[None] & in_seq[:, None, :]
    s = jnp.where(mask[None, :, None, None], s, -jnp.inf)
    sk = sink.astype(F32).reshape(Hkv, G)[None, None, :, :, None, None]
    m = jnp.maximum(jnp.max(s, axis=-1, keepdims=True), sk)
    p = jnp.exp(s - m)
    denom = jnp.sum(p, axis=-1, keepdims=True) + jnp.exp(sk - m)
    o = jnp.einsum('bnhgqk,bnkhd->bnqhgd', p / denom, vw)
    return o.reshape(B, S, Hkv * G * hd).astype(q.dtype)


def mem_attn(q, k, v):
    s = jnp.einsum('bshd,bmhd->bhsm', q.astype(F32), k.astype(F32)) * (q.shape[-1] ** -0.5)
    p = jax.nn.softmax(s, axis=-1)
    o = jnp.einsum('bhsm,bmhd->bshd', p, v.astype(F32))
    return o.astype(q.dtype)


def conv_glu(h, w_up, conv_w, conv_b, w_down):
    u = h @ w_up
    up = jnp.pad(u, ((0, 0), (1, 1), (0, 0)))
    c = up[:, :-2] * conv_w[0] + up[:, 1:-1] * conv_w[1] + up[:, 2:] * conv_w[2] + conv_b
    gate, val = jnp.split(c, 2, axis=-1)
    return (jax.nn.silu(gate) * val) @ w_down


def trunk(x, mem, norm_mix, norm_mem, norm_ffn, hg_w_in, hg_lb, hg_gn,
          gq_w_in, gq_qn, gq_kn, gq_sink, x_w_kv, x_qn, x_kn, w_out,
          ffn_w_up, ffn_conv_w, ffn_conv_b, ffn_w_down):
    B, S, _ = x.shape
    pos = jnp.arange(S, dtype=F32)
    lb_all = jnp.cumsum(jax.nn.softmax(hg_lb.astype(F32), axis=1), axis=1)
    for i in range(DEPTH):
        h = rmsnorm(x, norm_mix[i])
        mem_h = rmsnorm(mem, norm_mem[i])
        kv = (mem_h @ x_w_kv[i]).reshape(B, N_MEM, 2, X_HEADS, X_HEAD_DIM)
        mk = rmsnorm(kv[:, :, 0], x_kn[i])
        mv = kv[:, :, 1]
        if i % 2 == 0:
            a = i // 2
            cols = h @ hg_w_in[a]
            mix_out = hgrn2_mixer(cols[..., : 5 * MIX_WIDTH], lb_all[0, i], lb_all[1, i], hg_gn[a])
            xq = cols[..., 5 * MIX_WIDTH:]
        else:
            b = i // 2
            cols = h @ gq_w_in[b]
            q = cols[..., :MIX_WIDTH].reshape(B, S, N_Q_HEADS, HEAD_DIM)
            k = cols[..., MIX_WIDTH: MIX_WIDTH + KV_WIDTH].reshape(B, S, N_KV_HEADS, HEAD_DIM)
            v = cols[..., MIX_WIDTH + KV_WIDTH: MIX_WIDTH + 2 * KV_WIDTH].reshape(B, S, N_KV_HEADS, HEAD_DIM)
            xq = cols[..., MIX_WIDTH + 2 * KV_WIDTH:]
            q = rope(rmsnorm(q, gq_qn[b]), pos).reshape(B, S, N_KV_HEADS, GROUP, HEAD_DIM)
            k = rope(rmsnorm(k, gq_kn[b]), pos)
            mix_out = window_attn(q, k, v, gq_sink[b])
        xq = rmsnorm(xq.reshape(B, S, X_HEADS, X_HEAD_DIM), x_qn[i])
        x_out = mem_attn(xq, mk, mv).reshape(B, S, X_WIDTH)
        x = x + jnp.concatenate([mix_out, x_out], axis=-1) @ w_out[i]
        h2 = rmsnorm(x, norm_ffn[i])
        x = x + conv_glu(h2, ffn_w_up[i], ffn_conv_w[i], ffn_conv_b[i], ffn_w_down[i])
    return x


def setup_inputs(seed: int = 0) -> dict:
    key = jax.random.key(seed)
    ks = jax.random.split(key, 24)
    nrm = lambda k, shape, s: jax.random.normal(k, shape, F32) * s
    gain = lambda k, shape: 1.0 + 0.02 * jax.random.normal(k, shape, F32)
    D = D_MODEL
    return {
        "x_prompt": nrm(ks[0], (BATCH, SEQ, D), 1.0),
        "x_sample": nrm(ks[1], (DEC_BATCH, DEC_SEQ, D), 1.0),
        "mem_prompt": nrm(ks[2], (BATCH, N_MEM, D), 1.0),
        "mem_sample": nrm(ks[3], (DEC_BATCH, N_MEM, D), 1.0),
        "norm_mix": gain(ks[4], (DEPTH, D)),
        "norm_mem": gain(ks[5], (DEPTH, D)),
        "norm_ffn": gain(ks[6], (DEPTH, D)),
        "hg_w_in": nrm(ks[7], (N_A, D, HG_IN), D ** -0.5),
        "hg_lb": nrm(ks[8], (2, DEPTH + 1, MIX_WIDTH), 0.5),
        "hg_gn": gain(ks[9], (N_A, MIX_WIDTH)),
        "gq_w_in": nrm(ks[10], (N_B, D, GQA_IN), D ** -0.5),
        "gq_qn": gain(ks[11], (N_B, HEAD_DIM)),
        "gq_kn": gain(ks[12], (N_B, HEAD_DIM)),
        "gq_sink": nrm(ks[13], (N_B, N_Q_HEADS), 0.5),
        "x_w_kv": nrm(ks[14], (DEPTH, D, 2 * X_WIDTH), D ** -0.5),
        "x_qn": gain(ks[15], (DEPTH, X_HEAD_DIM)),
        "x_kn": gain(ks[16], (DEPTH, X_HEAD_DIM)),
        "w_out": nrm(ks[17], (DEPTH, OUT_IN, D), OUT_IN ** -0.5),
        "ffn_w_up": nrm(ks[18], (DEPTH, D, 2 * D_FF), D ** -0.5),
        "ffn_conv_w": nrm(ks[19], (DEPTH, CONV_W, 2 * D_FF), CONV_W ** -0.5),
        "ffn_conv_b": nrm(ks[20], (DEPTH, 2 * D_FF), 0.02),
        "ffn_w_down": nrm(ks[21], (DEPTH, D_FF, D), D_FF ** -0.5),
    }


def reference(x_prompt, x_sample, mem_prompt, mem_sample, norm_mix, norm_mem, norm_ffn,
              hg_w_in, hg_lb, hg_gn, gq_w_in, gq_qn, gq_kn, gq_sink, x_w_kv, x_qn, x_kn,
              w_out, ffn_w_up, ffn_conv_w, ffn_conv_b, ffn_w_down):
    y_prompt = trunk(x_prompt, mem_prompt, norm_mix, norm_mem, norm_ffn, hg_w_in, hg_lb, hg_gn,
                     gq_w_in, gq_qn, gq_kn, gq_sink, x_w_kv, x_qn, x_kn, w_out,
                     ffn_w_up, ffn_conv_w, ffn_conv_b, ffn_w_down)
    y_sample = trunk(x_sample, mem_sample, norm_mix, norm_mem, norm_ffn, hg_w_in, hg_lb, hg_gn,
                     gq_w_in, gq_qn, gq_kn, gq_sink, x_w_kv, x_qn, x_kn, w_out,
                     ffn_w_up, ffn_conv_w, ffn_conv_b, ffn_w_down)
    return (y_prompt, y_sample)
```

```python
import functools

import jax
import jax.numpy as jnp
from jax import lax
from jax.experimental import pallas as pl
from jax.experimental.pallas import tpu as pltpu

F32 = jnp.float32
BF16 = jnp.bfloat16

D_MODEL = 1024
DEPTH = 2
MIX = 768
N_MEM = 256
X_HEADS = 4
X_HD = 64
X_W = X_HEADS * X_HD
HG_HEADS = 6
HG_DK = 128
HG_CHUNK = 64
N_Q = 12
N_KV = 4
HD = 64
GROUP = N_Q // N_KV
KV_W = N_KV * HD
WINDOW = 128
ROPE_THETA = 10000.0
D_FF = 2816
EPS = 1e-6

LANES = 128
MXU_N = 256
VMEM_LIMIT = 48 * 1024 * 1024

TM_HG = 256
TM_PROJ = 256
TM_ATT = 256
TM_FFN = 512
FFN_HALO = 16
FFN_TN = 256


def _params(n_axes):
    return pltpu.CompilerParams(
        dimension_semantics=("arbitrary",) * n_axes, vmem_limit_bytes=VMEM_LIMIT)


def _rms_rows(x, w):
    ms = jnp.mean(x * x, axis=-1, keepdims=True)
    return x * lax.rsqrt(ms + EPS) * w


def _head_ones(width, hd):
    r = lax.broadcasted_iota(jnp.int32, (width, width), 0) // hd
    c = lax.broadcasted_iota(jnp.int32, (width, width), 1) // hd
    return jnp.where(r == c, 1.0, 0.0).astype(BF16)


def _head_rms(x, hd):
    ones = _head_ones(MXU_N, hd)
    parts = []
    for g in range(x.shape[-1] // MXU_N):
        xs = x[:, g * MXU_N:(g + 1) * MXU_N]
        ss = jnp.dot((xs * xs).astype(BF16), ones, preferred_element_type=F32)
        parts.append(xs * lax.rsqrt(ss * (1.0 / hd) + EPS))
    return parts[0] if len(parts) == 1 else jnp.concatenate(parts, axis=-1)


def _dot_nt(a, b):
    return lax.dot_general(a, b, (((1,), (1,)), ((), ())), preferred_element_type=F32)


def _dot_tn(a, b):
    return lax.dot_general(a, b, (((0,), (0,)), ((), ())), preferred_element_type=F32)


def _memkv_kernel(mem_ref, nw_ref, wkv_ref, kn_ref, mk_ref, mv_ref):
    mh = _rms_rows(mem_ref[0], nw_ref[0]).astype(BF16)
    kv = jnp.dot(mh, wkv_ref[0], preferred_element_type=F32)
    k = _head_rms(kv[:, :X_W], X_HD) * kn_ref[0]
    v = kv[:, X_W:]
    head = lax.broadcasted_iota(jnp.int32, (N_MEM, X_W), 1) // X_HD
    for h in range(X_HEADS):
        mk_ref[0, 0, h] = jnp.where(head == h, k, 0.0).astype(BF16)
        mv_ref[0, 0, h] = jnp.where(head == h, v, 0.0).astype(BF16)


def _memkv(mem, norm_mem, w_kv, kn_t):
    B = mem.shape[0]
    out = jax.ShapeDtypeStruct((DEPTH, B, X_HEADS, N_MEM, X_W), BF16)
    return pl.pallas_call(
        _memkv_kernel,
        grid=(DEPTH, B),
        in_specs=[
            pl.BlockSpec((1, N_MEM, D_MODEL), lambda l, b: (b, 0, 0)),
            pl.BlockSpec((1, 1, D_MODEL), lambda l, b: (l, 0, 0)),
            pl.BlockSpec((1, D_MODEL, 2 * X_W), lambda l, b: (l, 0, 0)),
            pl.BlockSpec((1, 1, X_W), lambda l, b: (l, 0, 0)),
        ],
        out_specs=[
            pl.BlockSpec((1, 1, X_HEADS, N_MEM, X_W), lambda l, b: (l, b, 0, 0, 0)),
            pl.BlockSpec((1, 1, X_HEADS, N_MEM, X_W), lambda l, b: (l, b, 0, 0, 0)),
        ],
        out_shape=[out, out],
        compiler_params=_params(2),
        name="mem_kv",
    )(mem, norm_mem, w_kv, kn_t)


def _mem_attn(xq, qn, mk_ref, mv_ref):
    q = ((_head_rms(xq, X_HD) * qn) * (X_HD ** -0.5)).astype(BF16)
    acc = None
    for h in range(X_HEADS):
        s = _dot_nt(q, mk_ref[h])
        m = jnp.max(s, axis=-1, keepdims=True)
        p = jnp.exp(s - m)
        pn = (p / jnp.sum(p, axis=-1, keepdims=True)).astype(BF16)
        o = jnp.dot(pn, mv_ref[h], preferred_element_type=F32)
        acc = o if acc is None else acc + o
    return acc


def _lower_bound(lb_ref, direction, layer):
    r = lb_ref[direction]
    e = jnp.exp(r - jnp.max(r, axis=0, keepdims=True))
    return jnp.sum(e[:layer + 1], axis=0, keepdims=True) / jnp.sum(e, axis=0, keepdims=True)


def _chunk_tri(n, reverse):
    r = lax.broadcasted_iota(jnp.int32, (n, n), 0)
    c = lax.broadcasted_iota(jnp.int32, (n, n), 1)
    same = (r // HG_CHUNK) == (c // HG_CHUNK)
    order = (c >= r) if reverse else (c <= r)
    return jnp.where(same & order, 1.0, 0.0).astype(BF16)


def _gla_block(qs, f_raw, lb, vb, st_ref, qt_ref, kt_ref, o_ref, reverse):
    R = qs.shape[0]
    f = lb + (1.0 - lb) * jax.nn.sigmoid(f_raw)
    g = jnp.log(f)
    k = 1.0 - f
    tri = _chunk_tri(R, reverse)
    g_hi = g.astype(BF16)
    g_lo = (g - g_hi.astype(F32)).astype(BF16)
    b = (jnp.dot(tri, g_hi, preferred_element_type=F32)
         + jnp.dot(tri, g_lo, preferred_element_type=F32))
    qt_ref[...] = (qs * jnp.exp(b)).astype(BF16)
    kt_ref[...] = (k * jnp.exp(-b)).astype(BF16)
    n_chunks = R // HG_CHUNK
    ri = lax.broadcasted_iota(jnp.int32, (HG_CHUNK, HG_CHUNK), 0)
    ci = lax.broadcasted_iota(jnp.int32, (HG_CHUNK, HG_CHUNK), 1)
    keep = (ci >= ri) if reverse else (ci <= ri)
    order = range(n_chunks - 1, -1, -1) if reverse else range(n_chunks)
    for c in order:
        r0 = c * HG_CHUNK
        end_row = r0 if reverse else r0 + HG_CHUNK - 1
        decay = jnp.exp(b[end_row:end_row + 1, :])
        for h in range(HG_HEADS):
            c0 = h * HG_DK
            qt = qt_ref[r0:r0 + HG_CHUNK, c0:c0 + HG_DK]
            kt = kt_ref[r0:r0 + HG_CHUNK, c0:c0 + HG_DK]
            vv = vb[r0:r0 + HG_CHUNK, c0:c0 + HG_DK]
            st = st_ref[c0:c0 + HG_DK, :]
            sc = jnp.where(keep, _dot_nt(qt, kt), 0.0).astype(BF16)
            o = jnp.dot(sc, vv, preferred_element_type=F32) + _dot_nt(qt, st.astype(BF16))
            o_ref[r0:r0 + HG_CHUNK, c0:c0 + HG_DK] = o
            st_ref[c0:c0 + HG_DK, :] = (st + _dot_tn(vv, kt)) * decay[:, c0:c0 + HG_DK]


def _hg_fwd_kernel(x_ref, nw_ref, w_ref, lb_ref, qn_ref, mk_ref, mv_ref,
                   qs_ref, v_ref, fb_ref, sg_ref, of_ref, xo_ref,
                   st_ref, qt_ref, kt_ref):
    @pl.when(pl.program_id(1) == 0)
    def _():
        st_ref[...] = jnp.zeros_like(st_ref)

    h = _rms_rows(x_ref[0], nw_ref[...]).astype(BF16)
    proj = lambda i, w=MIX: jnp.dot(h, w_ref[:, i * MIX:i * MIX + w], preferred_element_type=F32)
    q = proj(0)
    qs = q * jax.nn.sigmoid(q)
    qs_ref[0] = qs.astype(BF16)
    fb_ref[0] = proj(2)
    vb = proj(3).astype(BF16)
    v_ref[0] = vb
    go = proj(4)
    sg_ref[0] = (go * jax.nn.sigmoid(go)).astype(BF16)
    xo_ref[0] = _mem_attn(proj(5, X_W), qn_ref[...], mk_ref.at[0, 0], mv_ref.at[0, 0]).astype(BF16)
    _gla_block(qs, proj(1), _lower_bound(lb_ref, 0, 0), vb, st_ref, qt_ref, kt_ref,
               of_ref.at[0], reverse=False)


def _hg_bwd_kernel(x_ref, qs_ref, v_ref, fb_ref, sg_ref, of_ref, xo_ref, lb_ref, gn_ref, wo_ref,
                   y_ref, st_ref, qt_ref, kt_ref, ob_ref):
    @pl.when(pl.program_id(1) == 0)
    def _():
        st_ref[...] = jnp.zeros_like(st_ref)

    _gla_block(qs_ref[0].astype(F32), fb_ref[0], _lower_bound(lb_ref, 1, 0), v_ref[0],
               st_ref, qt_ref, kt_ref, ob_ref, reverse=True)
    o = of_ref[0] + ob_ref[...]
    on = (_rms_rows(o, gn_ref[...]) * sg_ref[0].astype(F32)).astype(BF16)
    y_ref[0] = (x_ref[0]
                + jnp.dot(on, wo_ref[:MIX, :], preferred_element_type=F32)
                + jnp.dot(xo_ref[0], wo_ref[MIX:, :], preferred_element_type=F32))


def _hgrn2_layer(x, nw, w_in, hg_lb, gn, qn_t, mk, mv, w_out):
    B, S, _ = x.shape
    tm = TM_HG
    nb = S // tm
    tok = lambda w: pl.BlockSpec((1, tm, w), lambda b, j: (b, j, 0))
    full = lambda a: pl.BlockSpec(a.shape, lambda b, j: (0,) * a.ndim)
    memspec = pl.BlockSpec((1, 1, X_HEADS, N_MEM, X_W), lambda b, j: (0, b, 0, 0, 0))
    act = lambda w, dt: jax.ShapeDtypeStruct((B, S, w), dt)
    qs, v, fb, sg, of, xo = pl.pallas_call(
        _hg_fwd_kernel,
        grid=(B, nb),
        in_specs=[tok(D_MODEL), full(nw), full(w_in), full(hg_lb), full(qn_t), memspec, memspec],
        out_specs=[tok(MIX), tok(MIX), tok(MIX), tok(MIX), tok(MIX), tok(X_W)],
        out_shape=[act(MIX, BF16), act(MIX, BF16), act(MIX, F32), act(MIX, BF16),
                   act(MIX, F32), act(X_W, BF16)],
        scratch_shapes=[pltpu.VMEM((MIX, HG_DK), F32), pltpu.VMEM((tm, MIX), BF16),
                        pltpu.VMEM((tm, MIX), BF16)],
        compiler_params=_params(2),
        name="hgrn2_fwd",
    )(x, nw, w_in, hg_lb, qn_t, mk, mv)

    rtok = lambda w: pl.BlockSpec((1, tm, w), lambda b, j: (b, nb - 1 - j, 0))
    return pl.pallas_call(
        _hg_bwd_kernel,
        grid=(B, nb),
        in_specs=[rtok(D_MODEL), rtok(MIX), rtok(MIX), rtok(MIX), rtok(MIX), rtok(MIX), rtok(X_W),
                  full(hg_lb), full(gn), full(w_out)],
        out_specs=rtok(D_MODEL),
        out_shape=act(D_MODEL, F32),
        scratch_shapes=[pltpu.VMEM((MIX, HG_DK), F32), pltpu.VMEM((tm, MIX), BF16),
                        pltpu.VMEM((tm, MIX), BF16), pltpu.VMEM((tm, MIX), F32)],
        compiler_params=_params(2),
        name="hgrn2_bwd_out",
    )(x, qs, v, fb, sg, of, xo, hg_lb, gn, w_out)


def _ffn_kernel(x_ref, xp_ref, xn_ref, nw_ref, wu_ref, cw_ref, cb_ref, wd_ref, y_ref,
                h_ref, ug_ref, uv_ref, a_ref):
    j = pl.program_id(1)
    nb = pl.num_programs(1)
    tm = x_ref.shape[1]
    nw = nw_ref[...]
    x = x_ref[0]
    hp = _rms_rows(xp_ref[0], nw) * jnp.where(j > 0, 1.0, 0.0)
    hn = _rms_rows(xn_ref[0], nw) * jnp.where(j < nb - 1, 1.0, 0.0)
    h_ref[0:FFN_HALO, :] = hp.astype(BF16)
    h_ref[FFN_HALO:FFN_HALO + tm, :] = _rms_rows(x, nw).astype(BF16)
    h_ref[FFN_HALO + tm:, :] = hn.astype(BF16)
    hh = h_ref[...]

    def conv(u_ref, c0):
        w = cw_ref[:, c0:c0 + FFN_TN]
        return (u_ref[FFN_HALO - 1:FFN_HALO - 1 + tm, :] * w[0:1]
                + u_ref[FFN_HALO:FFN_HALO + tm, :] * w[1:2]
                + u_ref[FFN_HALO + 1:FFN_HALO + 1 + tm, :] * w[2:3]
                + cb_ref[:, c0:c0 + FFN_TN])

    for t in range(D_FF // FFN_TN):
        c0 = t * FFN_TN
        ug_ref[...] = jnp.dot(hh, wu_ref[:, c0:c0 + FFN_TN], preferred_element_type=F32)
        uv_ref[...] = jnp.dot(hh, wu_ref[:, D_FF + c0:D_FF + c0 + FFN_TN],
                              preferred_element_type=F32)
        gate = conv(ug_ref, c0)
        val = conv(uv_ref, D_FF + c0)
        a_ref[:, c0:c0 + FFN_TN] = (gate * jax.nn.sigmoid(gate) * val).astype(BF16)
    y_ref[0] = x + jnp.dot(a_ref[...], wd_ref[...], preferred_element_type=F32)


def _ffn(x, nw, w_up, conv_w, conv_b, w_down):
    B, S, _ = x.shape
    tm = TM_FFN
    nb = S // tm
    r = tm // FFN_HALO
    last = S // FFN_HALO - 1
    full = lambda a: pl.BlockSpec(a.shape, lambda b, j: (0,) * a.ndim)
    return pl.pallas_call(
        _ffn_kernel,
        grid=(B, nb),
        in_specs=[
            pl.BlockSpec((1, tm, D_MODEL), lambda b, j: (b, j, 0)),
            pl.BlockSpec((1, FFN_HALO, D_MODEL), lambda b, j: (b, jnp.maximum(j * r - 1, 0), 0)),
            pl.BlockSpec((1, FFN_HALO, D_MODEL),
                         lambda b, j: (b, jnp.minimum((j + 1) * r, last), 0)),
            full(nw), full(w_up), full(conv_w), full(conv_b), full(w_down),
        ],
        out_specs=pl.BlockSpec((1, tm, D_MODEL), lambda b, j: (b, j, 0)),
        out_shape=jax.ShapeDtypeStruct((B, S, D_MODEL), F32),
        scratch_shapes=[pltpu.VMEM((tm + 2 * FFN_HALO, D_MODEL), BF16),
                        pltpu.VMEM((tm + 2 * FFN_HALO, FFN_TN), F32),
                        pltpu.VMEM((tm + 2 * FFN_HALO, FFN_TN), F32),
                        pltpu.VMEM((tm, D_FF), BF16)],
        compiler_params=_params(2),
        name="conv_glu",
    )(x, x, x, nw, w_up, conv_w, conv_b, w_down)


def _rope(x, cos, sin_signed, first_half):
    rot = jnp.where(first_half, pltpu.roll(x, LANES - HD // 2, 1), pltpu.roll(x, HD // 2, 1))
    return x * cos + rot * sin_signed


def _gq_proj_kernel(x_ref, nw_ref, w_ref, qn_ref, kn_ref, cos_ref, sin_ref, xqn_ref,
                    mk_ref, mv_ref, q_ref, k_ref, v_ref, xo_ref):
    h = _rms_rows(x_ref[0], nw_ref[...]).astype(BF16)
    cols = jnp.dot(h, w_ref[...], preferred_element_type=F32)
    cos = cos_ref[...]
    sin = sin_ref[...]
    first_half = (lax.broadcasted_iota(jnp.int32, cos.shape, 1) % HD) < HD // 2
    qn = _head_rms(cols[:, :MIX], HD) * qn_ref[...]
    for g in range(MIX // LANES):
        sl = slice(g * LANES, (g + 1) * LANES)
        q_ref[0, :, sl] = (_rope(qn[:, sl], cos, sin, first_half) * (HD ** -0.5)).astype(BF16)
    kn = _head_rms(cols[:, MIX:MIX + KV_W], HD) * kn_ref[...]
    v = cols[:, MIX + KV_W:MIX + 2 * KV_W]
    for g in range(KV_W // LANES):
        sl = slice(g * LANES, (g + 1) * LANES)
        sw = slice(KV_W + g * LANES, KV_W + (g + 1) * LANES)
        kr = _rope(kn[:, sl], cos, sin, first_half)
        k_ref[0, :, sl] = kr.astype(BF16)
        k_ref[0, :, sw] = pltpu.roll(kr, HD, 1).astype(BF16)
        v_ref[0, :, sl] = v[:, sl].astype(BF16)
        v_ref[0, :, sw] = pltpu.roll(v[:, sl], HD, 1).astype(BF16)
    xo_ref[0] = _mem_attn(cols[:, MIX + 2 * KV_W:], xqn_ref[...],
                          mk_ref.at[0, 0], mv_ref.at[0, 0]).astype(BF16)


def _win_attn_kernel(sink_ref, x_ref, q_ref, k_ref, kp_ref, kn_ref, v_ref, vp_ref, vn_ref,
                     xo_ref, wo_ref, y_ref, kv_ref, vv_ref, bias_ref, o_ref, *, seq_len):
    j = pl.program_id(1)
    tm = x_ref.shape[1]
    rows = tm + 2 * WINDOW
    half = lax.broadcasted_iota(jnp.int32, (rows, LANES), 1) // HD
    for src, prev, nxt, dst in ((k_ref, kp_ref, kn_ref, kv_ref), (v_ref, vp_ref, vn_ref, vv_ref)):
        for kvh in range(N_KV):
            for hq in range(2):
                c0 = (kvh // 2) * LANES + (0 if kvh % 2 == hq else KV_W)
                cat = jnp.concatenate([prev[0, :, c0:c0 + LANES], src[0, :, c0:c0 + LANES],
                                       nxt[0, :, c0:c0 + LANES]], axis=0)
                dst[kvh * 2 + hq] = jnp.where(half == hq, cat, jnp.zeros_like(cat))

    qi = lax.broadcasted_iota(jnp.int32, (WINDOW, 3 * WINDOW), 0)
    kj = lax.broadcasted_iota(jnp.int32, (WINDOW, 3 * WINDOW), 1)
    for i in range(tm // WINDOW):
        kpos = j * tm + (i - 1) * WINDOW + kj
        ok = (kj - qi >= 0) & (kj - qi <= 2 * WINDOW) & (kpos >= 0) & (kpos < seq_len)
        bias_ref[...] = jnp.where(ok, 0.0, -jnp.inf)
        r0 = i * WINDOW
        for g in range(N_Q // 2):
            qp = q_ref[0, r0:r0 + WINDOW, g * LANES:(g + 1) * LANES]
            acc = None
            for hq in range(2):
                hd_idx = 2 * g + hq
                slab = (hd_idx // GROUP) * 2 + hq
                s = _dot_nt(qp, kv_ref[slab, r0:r0 + 3 * WINDOW, :]) + bias_ref[...]
                sk = sink_ref[hd_idx]
                m = jnp.maximum(jnp.max(s, axis=-1, keepdims=True), sk)
                p = jnp.exp(s - m)
                den = jnp.sum(p, axis=-1, keepdims=True) + jnp.exp(sk - m)
                o = jnp.dot((p / den).astype(BF16), vv_ref[slab, r0:r0 + 3 * WINDOW, :],
                            preferred_element_type=F32)
                acc = o if acc is None else acc + o
            o_ref[r0:r0 + WINDOW, g * LANES:(g + 1) * LANES] = acc.astype(BF16)
    y_ref[0] = (x_ref[0]
                + jnp.dot(o_ref[...], wo_ref[:MIX, :], preferred_element_type=F32)
                + jnp.dot(xo_ref[0], wo_ref[MIX:, :], preferred_element_type=F32))


def _gqa_layer(x, nw, w_in, qn_t, kn_t, sink, cos, sin_signed, xqn_t, mk, mv, w_out):
    B, S, _ = x.shape
    tm = TM_PROJ
    tok = lambda w, t=tm: pl.BlockSpec((1, t, w), lambda b, j: (b, j, 0))
    full = lambda a: pl.BlockSpec(a.shape, lambda b, j: (0,) * a.ndim)
    memspec = pl.BlockSpec((1, 1, X_HEADS, N_MEM, X_W), lambda b, j: (1, b, 0, 0, 0))
    act = lambda w, dt: jax.ShapeDtypeStruct((B, S, w), dt)
    q, k2, v2, xo = pl.pallas_call(
        _gq_proj_kernel,
        grid=(B, S // tm),
        in_specs=[tok(D_MODEL), full(nw), full(w_in), full(qn_t), full(kn_t),
                  pl.BlockSpec((tm, LANES), lambda b, j: (j, 0)),
                  pl.BlockSpec((tm, LANES), lambda b, j: (j, 0)),
                  full(xqn_t), memspec, memspec],
        out_specs=[tok(MIX), tok(2 * KV_W), tok(2 * KV_W), tok(X_W)],
        out_shape=[act(MIX, BF16), act(2 * KV_W, BF16), act(2 * KV_W, BF16), act(X_W, BF16)],
        compiler_params=_params(2),
        name="gqa_proj",
    )(x, nw, w_in, qn_t, kn_t, cos, sin_signed, xqn_t, mk, mv)

    ta = TM_ATT
    r = ta // WINDOW
    last = S // WINDOW - 1
    prev = pl.BlockSpec((1, WINDOW, 2 * KV_W), lambda b, j: (b, jnp.maximum(j * r - 1, 0), 0))
    nxt = pl.BlockSpec((1, WINDOW, 2 * KV_W), lambda b, j: (b, jnp.minimum((j + 1) * r, last), 0))
    return pl.pallas_call(
        functools.partial(_win_attn_kernel, seq_len=S),
        grid=(B, S // ta),
        in_specs=[pl.BlockSpec(memory_space=pltpu.SMEM),
                  tok(D_MODEL, ta), tok(MIX, ta),
                  tok(2 * KV_W, ta), prev, nxt, tok(2 * KV_W, ta), prev, nxt,
                  tok(X_W, ta), full(w_out)],
        out_specs=tok(D_MODEL, ta),
        out_shape=act(D_MODEL, F32),
        scratch_shapes=[pltpu.VMEM((2 * N_KV, ta + 2 * WINDOW, LANES), BF16),
                        pltpu.VMEM((2 * N_KV, ta + 2 * WINDOW, LANES), BF16),
                        pltpu.VMEM((WINDOW, 3 * WINDOW), F32),
                        pltpu.VMEM((ta, MIX), BF16)],
        compiler_params=_params(2),
        name="win_attn_out",
    )(sink, x, q, k2, k2, k2, v2, v2, v2, xo, w_out)


def _rope_tables(seq_len):
    lane = jnp.arange(LANES) % HD
    freqs = ROPE_THETA ** (-(2 * (lane % (HD // 2))).astype(F32) / HD)
    ang = jnp.arange(seq_len, dtype=F32)[:, None] * freqs[None, :]
    sign = jnp.where(lane < HD // 2, -1.0, 1.0)
    return jnp.cos(ang), jnp.sin(ang) * sign


def _trunk(x, mem, p):
    S = x.shape[1]
    mk, mv = _memkv(mem, p["norm_mem"], p["x_w_kv"], p["x_kn"])
    cos, sin_signed = _rope_tables(S)
    for i in range(DEPTH):
        if i % 2 == 0:
            a = i // 2
            assert a == 0 and i == 0, "the HGRN2 kernels read layer 0's forget-gate lower bounds"
            x = _hgrn2_layer(x, p["norm_mix"][i], p["hg_w_in"][a], p["hg_lb"], p["hg_gn"][a],
                             p["x_qn"][i], mk, mv, p["w_out"][i])
        else:
            b = i // 2
            assert i == 1, "the attention kernels read layer 1's memory keys and values"
            x = _gqa_layer(x, p["norm_mix"][i], p["gq_w_in"][b], p["gq_qn"][b], p["gq_kn"][b],
                           p["gq_sink"][b], cos, sin_signed, p["x_qn"][i], mk, mv, p["w_out"][i])
        x = _ffn(x, p["norm_ffn"][i], p["ffn_w_up"][i], p["ffn_conv_w"][i], p["ffn_conv_b"][i],
                 p["ffn_w_down"][i])
    return x


def _prepare(norm_mix, norm_mem, norm_ffn, hg_w_in, hg_lb, hg_gn, gq_w_in, gq_qn, gq_kn, gq_sink,
             x_w_kv, x_qn, x_kn, w_out, ffn_w_up, ffn_conv_w, ffn_conv_b, ffn_w_down):
    row = lambda a: a.reshape(a.shape[0], 1, a.shape[-1])
    return {
        "norm_mix": row(norm_mix), "norm_mem": row(norm_mem), "norm_ffn": row(norm_ffn),
        "hg_w_in": hg_w_in.astype(BF16), "hg_lb": hg_lb, "hg_gn": row(hg_gn),
        "gq_w_in": gq_w_in.astype(BF16),
        "gq_qn": row(jnp.tile(gq_qn, (1, N_Q))), "gq_kn": row(jnp.tile(gq_kn, (1, N_KV))),
        "gq_sink": gq_sink,
        "x_w_kv": x_w_kv.astype(BF16),
        "x_qn": row(jnp.tile(x_qn, (1, X_HEADS))), "x_kn": row(jnp.tile(x_kn, (1, X_HEADS))),
        "w_out": w_out.astype(BF16),
        "ffn_w_up": ffn_w_up.astype(BF16), "ffn_conv_w": ffn_conv_w, "ffn_conv_b": row(ffn_conv_b),
        "ffn_w_down": ffn_w_down.astype(BF16),
    }


def kernel(x_prompt, x_sample, mem_prompt, mem_sample, norm_mix, norm_mem, norm_ffn, hg_w_in, hg_lb, hg_gn, gq_w_in, gq_qn, gq_kn, gq_sink, x_w_kv, x_qn, x_kn, w_out, ffn_w_up, ffn_conv_w, ffn_conv_b, ffn_w_down):
    p = _prepare(norm_mix, norm_mem, norm_ffn, hg_w_in, hg_lb, hg_gn, gq_w_in, gq_qn, gq_kn,
                 gq_sink, x_w_kv, x_qn, x_kn, w_out, ffn_w_up, ffn_conv_w, ffn_conv_b, ffn_w_down)
    return (_trunk(x_prompt, mem_prompt, p), _trunk(x_sample, mem_sample, p))
```

```python
import functools

import jax
import jax.numpy as jnp
from jax import lax
from jax.experimental import pallas as pl
from jax.experimental.pallas import tpu as pltpu

F32 = jnp.float32
BF16 = jnp.bfloat16

D_MODEL = 1024
DEPTH = 2
MIX = 768
N_MEM = 256
X_HEADS = 4
X_HD = 64
X_W = X_HEADS * X_HD
HG_HEADS = 6
HG_DK = 128
HG_CHUNK = 64
N_Q = 12
N_KV = 4
HD = 64
GROUP = N_Q // N_KV
KV_W = N_KV * HD
WINDOW = 128
ROPE_THETA = 10000.0
D_FF = 2816
EPS = 1e-6

LANES = 128
MXU_N = 256
VMEM_LIMIT = 48 * 1024 * 1024

TM_HG = 256
TM_PROJ = 256
TM_ATT = 256
TM_FFN = 512
FFN_HALO = 8
FFN_TN = 256


MOSAIC_FLAGS = {}


def _params(n_axes):
    return pltpu.CompilerParams(
        dimension_semantics=("arbitrary",) * n_axes, vmem_limit_bytes=VMEM_LIMIT,
        flags=dict(MOSAIC_FLAGS) or None)


def _rms_rows(x, w):
    ms = jnp.mean(x * x, axis=-1, keepdims=True)
    return x * lax.rsqrt(ms + EPS) * w


def _head_ones(width, hd):
    r = lax.broadcasted_iota(jnp.int32, (width, width), 0) // hd
    c = lax.broadcasted_iota(jnp.int32, (width, width), 1) // hd
    return jnp.where(r == c, 1.0, 0.0).astype(BF16)


def _head_rms(x, hd):
    ones = _head_ones(MXU_N, hd)
    parts = []
    for g in range(x.shape[-1] // MXU_N):
        xs = x[:, g * MXU_N:(g + 1) * MXU_N]
        ss = jnp.dot((xs * xs).astype(BF16), ones, preferred_element_type=F32)
        parts.append(xs * lax.rsqrt(ss * (1.0 / hd) + EPS))
    return parts[0] if len(parts) == 1 else jnp.concatenate(parts, axis=-1)


def _dot_nt(a, b):
    return lax.dot_general(a, b, (((1,), (1,)), ((), ())), preferred_element_type=F32)


def _dot_tn(a, b):
    return lax.dot_general(a, b, (((0,), (0,)), ((), ())), preferred_element_type=F32)


def _memkv_kernel(mem_ref, nw_ref, wkv_ref, kn_ref, mk_ref, mv_ref):
    mh = _rms_rows(mem_ref[0], nw_ref[0]).astype(BF16)
    kv = jnp.dot(mh, wkv_ref[0], preferred_element_type=F32)
    k = _head_rms(kv[:, :X_W], X_HD) * kn_ref[0]
    v = kv[:, X_W:]
    head = lax.broadcasted_iota(jnp.int32, (N_MEM, X_W), 1) // X_HD
    for h in range(X_HEADS):
        mk_ref[0, 0, h] = jnp.where(head == h, k, 0.0).astype(BF16)
        mv_ref[0, 0, h] = jnp.where(head == h, v, 0.0).astype(BF16)


def _memkv(mem, norm_mem, w_kv, kn_t):
    B = mem.shape[0]
    out = jax.ShapeDtypeStruct((DEPTH, B, X_HEADS, N_MEM, X_W), BF16)
    return pl.pallas_call(
        _memkv_kernel,
        grid=(DEPTH, B),
        in_specs=[
            pl.BlockSpec((1, N_MEM, D_MODEL), lambda l, b: (b, 0, 0)),
            pl.BlockSpec((1, 1, D_MODEL), lambda l, b: (l, 0, 0)),
            pl.BlockSpec((1, D_MODEL, 2 * X_W), lambda l, b: (l, 0, 0)),
            pl.BlockSpec((1, 1, X_W), lambda l, b: (l, 0, 0)),
        ],
        out_specs=[
            pl.BlockSpec((1, 1, X_HEADS, N_MEM, X_W), lambda l, b: (l, b, 0, 0, 0)),
            pl.BlockSpec((1, 1, X_HEADS, N_MEM, X_W), lambda l, b: (l, b, 0, 0, 0)),
        ],
        out_shape=[out, out],
        compiler_params=_params(2),
        name="mem_kv",
    )(mem, norm_mem, w_kv, kn_t)


def _mem_scores(xq, qn, mk_ref):
    q = ((_head_rms(xq, X_HD) * qn) * (X_HD ** -0.5)).astype(BF16)
    return [_dot_nt(q, mk_ref[h]) for h in range(X_HEADS)]


def _mem_out(scores, mv_ref):
    R = scores[0].shape[0]
    s = jnp.concatenate(scores, axis=0)
    p = jnp.exp(s - jnp.max(s, axis=-1, keepdims=True))
    pn = (p / jnp.sum(p, axis=-1, keepdims=True)).astype(BF16)
    acc = None
    for h in range(len(scores)):
        o = jnp.dot(pn[h * R:(h + 1) * R], mv_ref[h], preferred_element_type=F32)
        acc = o if acc is None else acc + o
    return acc


def _lower_bound(lb_ref, direction, layer):
    r = lb_ref[direction]
    e = jnp.exp(r - jnp.max(r, axis=0, keepdims=True))
    return jnp.sum(e[:layer + 1], axis=0, keepdims=True) / jnp.sum(e, axis=0, keepdims=True)


def _chunk_tri(n, reverse):
    r = lax.broadcasted_iota(jnp.int32, (n, n), 0)
    c = lax.broadcasted_iota(jnp.int32, (n, n), 1)
    same = (r // HG_CHUNK) == (c // HG_CHUNK)
    order = (c >= r) if reverse else (c <= r)
    return jnp.where(same & order, 1.0, 0.0).astype(BF16)


def _gla_prep(qs, f_raw, lb, qt_ref, kt_ref, reverse):
    R = qs.shape[0]
    f = lb + (1.0 - lb) * jax.nn.sigmoid(f_raw)
    g = jnp.log(f)
    k = 1.0 - f
    tri = _chunk_tri(R, reverse)
    g_hi = g.astype(BF16)
    g_lo = (g - g_hi.astype(F32)).astype(BF16)
    b = (jnp.dot(tri, g_hi, preferred_element_type=F32)
         + jnp.dot(tri, g_lo, preferred_element_type=F32))
    qt_ref[...] = (qs * jnp.exp(b)).astype(BF16)
    kt_ref[...] = (k * jnp.exp(-b)).astype(BF16)
    decays = []
    for c in range(R // HG_CHUNK):
        end_row = c * HG_CHUNK if reverse else (c + 1) * HG_CHUNK - 1
        decays.append(jnp.exp(b[end_row:end_row + 1, :]))
    return decays


def _gla_chunks(vb, decays, st_ref, qt_ref, kt_ref, o_ref, reverse):
    n_chunks = len(decays)
    ri = lax.broadcasted_iota(jnp.int32, (HG_CHUNK, HG_CHUNK), 0)
    ci = lax.broadcasted_iota(jnp.int32, (HG_CHUNK, HG_CHUNK), 1)
    keep = (ci >= ri) if reverse else (ci <= ri)
    order = list(range(n_chunks - 1, -1, -1) if reverse else range(n_chunks))

    def local(c):
        r0 = c * HG_CHUNK
        out = []
        for h in range(HG_HEADS):
            c0 = h * HG_DK
            qt = qt_ref[r0:r0 + HG_CHUNK, c0:c0 + HG_DK]
            kt = kt_ref[r0:r0 + HG_CHUNK, c0:c0 + HG_DK]
            vv = vb[r0:r0 + HG_CHUNK, c0:c0 + HG_DK]
            sc = jnp.where(keep, _dot_nt(qt, kt), 0.0).astype(BF16)
            out.append((qt, vv, sc, _dot_tn(vv, kt)))
        return out

    nxt = local(order[0])
    for n, c in enumerate(order):
        cur = nxt
        if n + 1 < n_chunks:
            nxt = local(order[n + 1])
        r0 = c * HG_CHUNK
        for h, (qt, vv, sc, kv) in enumerate(cur):
            c0 = h * HG_DK
            st = st_ref[c0:c0 + HG_DK, :]
            o_ref[r0:r0 + HG_CHUNK, c0:c0 + HG_DK] = (
                jnp.dot(sc, vv, preferred_element_type=F32) + _dot_nt(qt, st.astype(BF16)))
            st_ref[c0:c0 + HG_DK, :] = (st + kv) * decays[c][:, c0:c0 + HG_DK]


def _hg_fwd_kernel(x_ref, nw_ref, w_ref, lb_ref, qn_ref, mk_ref, mv_ref,
                   qs_ref, v_ref, fb_ref, sg_ref, of_ref, xo_ref,
                   st_ref, qt_ref, kt_ref):
    @pl.when(pl.program_id(1) == 0)
    def _():
        st_ref[...] = jnp.zeros_like(st_ref)

    h = _rms_rows(x_ref[0], nw_ref[...]).astype(BF16)
    proj = lambda i, w=MIX: jnp.dot(h, w_ref[:, i * MIX:i * MIX + w], preferred_element_type=F32)
    f_raw = proj(1)
    q = proj(0)
    cx = proj(5, X_W)
    vb = proj(3).astype(BF16)
    go = proj(4)
    fb_ref[0] = proj(2)
    qs = q * jax.nn.sigmoid(q)
    qs_ref[0] = qs.astype(BF16)
    v_ref[0] = vb
    sg_ref[0] = (go * jax.nn.sigmoid(go)).astype(BF16)
    mem_s = _mem_scores(cx, qn_ref[...], mk_ref.at[0, 0])
    decays = _gla_prep(qs, f_raw, _lower_bound(lb_ref, 0, 0), qt_ref, kt_ref, reverse=False)
    xo_ref[0] = _mem_out(mem_s, mv_ref.at[0, 0]).astype(BF16)
    _gla_chunks(vb, decays, st_ref, qt_ref, kt_ref, of_ref.at[0], reverse=False)


def _hg_bwd_kernel(x_ref, qs_ref, v_ref, fb_ref, sg_ref, of_ref, xo_ref, lb_ref, gn_ref, wo_ref,
                   y_ref, st_ref, qt_ref, kt_ref, ob_ref):
    @pl.when(pl.program_id(1) == 0)
    def _():
        st_ref[...] = jnp.zeros_like(st_ref)

    decays = _gla_prep(qs_ref[0].astype(F32), fb_ref[0], _lower_bound(lb_ref, 1, 0),
                       qt_ref, kt_ref, reverse=True)
    _gla_chunks(v_ref[0], decays, st_ref, qt_ref, kt_ref, ob_ref, reverse=True)
    o = of_ref[0] + ob_ref[...]
    on = (_rms_rows(o, gn_ref[...]) * sg_ref[0].astype(F32)).astype(BF16)
    y_ref[0] = (x_ref[0]
                + jnp.dot(on, wo_ref[:MIX, :], preferred_element_type=F32)
                + jnp.dot(xo_ref[0], wo_ref[MIX:, :], preferred_element_type=F32))


def _hgrn2_layer(x, nw, w_in, hg_lb, gn, qn_t, mk, mv, w_out):
    B, S, _ = x.shape
    tm = TM_HG
    nb = S // tm
    tok = lambda w: pl.BlockSpec((1, tm, w), lambda b, j: (b, j, 0))
    full = lambda a: pl.BlockSpec(a.shape, lambda b, j: (0,) * a.ndim)
    memspec = pl.BlockSpec((1, 1, X_HEADS, N_MEM, X_W), lambda b, j: (0, b, 0, 0, 0))
    act = lambda w, dt: jax.ShapeDtypeStruct((B, S, w), dt)
    qs, v, fb, sg, of, xo = pl.pallas_call(
        _hg_fwd_kernel,
        grid=(B, nb),
        in_specs=[tok(D_MODEL), full(nw), full(w_in), full(hg_lb), full(qn_t), memspec, memspec],
        out_specs=[tok(MIX), tok(MIX), tok(MIX), tok(MIX), tok(MIX), tok(X_W)],
        out_shape=[act(MIX, BF16), act(MIX, BF16), act(MIX, F32), act(MIX, BF16),
                   act(MIX, F32), act(X_W, BF16)],
        scratch_shapes=[pltpu.VMEM((MIX, HG_DK), F32), pltpu.VMEM((tm, MIX), BF16),
                        pltpu.VMEM((tm, MIX), BF16)],
        compiler_params=_params(2),
        name="hgrn2_fwd",
    )(x, nw, w_in, hg_lb, qn_t, mk, mv)

    rtok = lambda w: pl.BlockSpec((1, tm, w), lambda b, j: (b, nb - 1 - j, 0))
    return pl.pallas_call(
        _hg_bwd_kernel,
        grid=(B, nb),
        in_specs=[rtok(D_MODEL), rtok(MIX), rtok(MIX), rtok(MIX), rtok(MIX), rtok(MIX), rtok(X_W),
                  full(hg_lb), full(gn), full(w_out)],
        out_specs=rtok(D_MODEL),
        out_shape=act(D_MODEL, F32),
        scratch_shapes=[pltpu.VMEM((MIX, HG_DK), F32), pltpu.VMEM((tm, MIX), BF16),
                        pltpu.VMEM((tm, MIX), BF16), pltpu.VMEM((tm, MIX), F32)],
        compiler_params=_params(2),
        name="hgrn2_bwd_out",
    )(x, qs, v, fb, sg, of, xo, hg_lb, gn, w_out)


def _ffn_kernel(x_ref, xp_ref, xn_ref, nw_ref, wu_ref, cw_ref, cb_ref, wd_ref, y_ref,
                h_ref, u_ref, a_ref):
    j = pl.program_id(1)
    nb = pl.num_programs(1)
    tm = x_ref.shape[1]
    rows = tm + 2 * FFN_HALO
    nw = nw_ref[...]
    x = x_ref[0]
    halo = jnp.concatenate([xn_ref[0] * jnp.where(j < nb - 1, 1.0, 0.0),
                            xp_ref[0] * jnp.where(j > 0, 1.0, 0.0)], axis=0)
    h_ref[0:tm, :] = _rms_rows(x, nw).astype(BF16)
    h_ref[tm:, :] = _rms_rows(halo, nw).astype(BF16)
    hh = h_ref[...]

    def up(t):
        for half in range(2):
            c0 = half * D_FF + t * FFN_TN
            u_ref[t % 2, half] = jnp.dot(hh, wu_ref[:, c0:c0 + FFN_TN], preferred_element_type=F32)

    def conv(t, half):
        c0 = half * D_FF + t * FFN_TN
        u = u_ref[t % 2, half]
        w = cw_ref[:, c0:c0 + FFN_TN]
        return (pltpu.roll(u, 1, 0)[0:tm] * w[0:1] + u[0:tm] * w[1:2]
                + pltpu.roll(u, rows - 1, 0)[0:tm] * w[2:3] + cb_ref[:, c0:c0 + FFN_TN])

    n_tiles = D_FF // FFN_TN
    up(0)
    for t in range(n_tiles):
        if t + 1 < n_tiles:
            up(t + 1)
        gate = conv(t, 0)
        val = conv(t, 1)
        a_ref[:, t * FFN_TN:(t + 1) * FFN_TN] = (gate * jax.nn.sigmoid(gate) * val).astype(BF16)
    y_ref[0] = x + jnp.dot(a_ref[...], wd_ref[...], preferred_element_type=F32)


def _ffn(x, nw, w_up, conv_w, conv_b, w_down):
    B, S, _ = x.shape
    tm = TM_FFN
    nb = S // tm
    r = tm // FFN_HALO
    last = S // FFN_HALO - 1
    full = lambda a: pl.BlockSpec(a.shape, lambda b, j: (0,) * a.ndim)
    return pl.pallas_call(
        _ffn_kernel,
        grid=(B, nb),
        in_specs=[
            pl.BlockSpec((1, tm, D_MODEL), lambda b, j: (b, j, 0)),
            pl.BlockSpec((1, FFN_HALO, D_MODEL), lambda b, j: (b, jnp.maximum(j * r - 1, 0), 0)),
            pl.BlockSpec((1, FFN_HALO, D_MODEL),
                         lambda b, j: (b, jnp.minimum((j + 1) * r, last), 0)),
            full(nw), full(w_up), full(conv_w), full(conv_b), full(w_down),
        ],
        out_specs=pl.BlockSpec((1, tm, D_MODEL), lambda b, j: (b, j, 0)),
        out_shape=jax.ShapeDtypeStruct((B, S, D_MODEL), F32),
        scratch_shapes=[pltpu.VMEM((tm + 2 * FFN_HALO, D_MODEL), BF16),
                        pltpu.VMEM((2, 2, tm + 2 * FFN_HALO, FFN_TN), F32),
                        pltpu.VMEM((tm, D_FF), BF16)],
        compiler_params=_params(2),
        name="conv_glu",
    )(x, x, x, nw, w_up, conv_w, conv_b, w_down)


def _rope(x, cos, sin_signed, first_half):
    rot = jnp.where(first_half, pltpu.roll(x, LANES - HD // 2, 1), pltpu.roll(x, HD // 2, 1))
    return x * cos + rot * sin_signed


def _gq_proj_kernel(x_ref, nw_ref, w_ref, qn_ref, kn_ref, cos_ref, sin_ref, xqn_ref,
                    mk_ref, mv_ref, q_ref, k_ref, v_ref, xo_ref):
    h = _rms_rows(x_ref[0], nw_ref[...]).astype(BF16)
    cols = jnp.dot(h, w_ref[...], preferred_element_type=F32)
    cos = cos_ref[...]
    sin = sin_ref[...]
    first_half = (lax.broadcasted_iota(jnp.int32, cos.shape, 1) % HD) < HD // 2
    qn = _head_rms(cols[:, :MIX], HD) * qn_ref[...]
    for g in range(MIX // LANES):
        sl = slice(g * LANES, (g + 1) * LANES)
        q_ref[0, :, sl] = (_rope(qn[:, sl], cos, sin, first_half) * (HD ** -0.5)).astype(BF16)
    kn = _head_rms(cols[:, MIX:MIX + KV_W], HD) * kn_ref[...]
    v = cols[:, MIX + KV_W:MIX + 2 * KV_W]
    for g in range(KV_W // LANES):
        sl = slice(g * LANES, (g + 1) * LANES)
        sw = slice(KV_W + g * LANES, KV_W + (g + 1) * LANES)
        kr = _rope(kn[:, sl], cos, sin, first_half)
        k_ref[0, :, sl] = kr.astype(BF16)
        k_ref[0, :, sw] = pltpu.roll(kr, HD, 1).astype(BF16)
        v_ref[0, :, sl] = v[:, sl].astype(BF16)
        v_ref[0, :, sw] = pltpu.roll(v[:, sl], HD, 1).astype(BF16)
    mem_s = _mem_scores(cols[:, MIX + 2 * KV_W:], xqn_ref[...], mk_ref.at[0, 0])
    xo_ref[0] = _mem_out(mem_s, mv_ref.at[0, 0]).astype(BF16)


def _win_attn_kernel(sink_ref, x_ref, q_ref, k_ref, kp_ref, kn_ref, v_ref, vp_ref, vn_ref,
                     xo_ref, wo_ref, y_ref, kv_ref, vv_ref, bias_ref, o_ref, *, seq_len):
    j = pl.program_id(1)
    tm = x_ref.shape[1]
    rows = tm + 2 * WINDOW
    half = lax.broadcasted_iota(jnp.int32, (rows, LANES), 1) // HD
    for src, prev, nxt, dst in ((k_ref, kp_ref, kn_ref, kv_ref), (v_ref, vp_ref, vn_ref, vv_ref)):
        for kvh in range(N_KV):
            for hq in range(2):
                c0 = (kvh // 2) * LANES + (0 if kvh % 2 == hq else KV_W)
                cat = jnp.concatenate([prev[0, :, c0:c0 + LANES], src[0, :, c0:c0 + LANES],
                                       nxt[0, :, c0:c0 + LANES]], axis=0)
                dst[kvh * 2 + hq] = jnp.where(half == hq, cat, jnp.zeros_like(cat))

    qi = lax.broadcasted_iota(jnp.int32, (WINDOW, 3 * WINDOW), 0)
    kj = lax.broadcasted_iota(jnp.int32, (WINDOW, 3 * WINDOW), 1)
    n_sub = tm // WINDOW
    for i in range(n_sub):
        kpos = j * tm + (i - 1) * WINDOW + kj
        ok = (kj - qi >= 0) & (kj - qi <= 2 * WINDOW) & (kpos >= 0) & (kpos < seq_len)
        bias_ref[i] = jnp.where(ok, 0.0, -jnp.inf)

    slab = lambda g, hq: ((2 * g + hq) // GROUP) * 2 + hq

    def scores(i, g):
        r0 = i * WINDOW
        qp = q_ref[0, r0:r0 + WINDOW, g * LANES:(g + 1) * LANES]
        return [_dot_nt(qp, kv_ref[slab(g, hq), r0:r0 + 3 * WINDOW, :]) for hq in range(2)]

    top = lax.broadcasted_iota(jnp.int32, (2 * WINDOW, 1), 0) < WINDOW
    work = [(i, g) for i in range(n_sub) for g in range(N_Q // 2)]

    def stage_max(item, s_raw):
        i, g = item
        s = jnp.concatenate(s_raw, axis=0) + jnp.concatenate([bias_ref[i], bias_ref[i]], axis=0)
        sk = jnp.where(top, sink_ref[2 * g], sink_ref[2 * g + 1])
        return s, sk, jnp.maximum(jnp.max(s, axis=-1, keepdims=True), sk)

    def stage_exp(item, st):
        s, sk, m = st
        p = jnp.exp(s - m)
        return p, jnp.sum(p, axis=-1, keepdims=True) + jnp.exp(sk - m)

    def stage_out(item, st):
        i, g = item
        p, den = st
        r0 = i * WINDOW
        pn = (p / den).astype(BF16)
        acc = (jnp.dot(pn[:WINDOW], vv_ref[slab(g, 0), r0:r0 + 3 * WINDOW, :],
                       preferred_element_type=F32)
               + jnp.dot(pn[WINDOW:], vv_ref[slab(g, 1), r0:r0 + 3 * WINDOW, :],
                         preferred_element_type=F32))
        o_ref[r0:r0 + WINDOW, g * LANES:(g + 1) * LANES] = acc.astype(BF16)

    stages = [lambda item, _: scores(*item), stage_max, stage_exp, stage_out]
    inflight = [None] * len(stages)
    for tick in range(len(work) + len(stages) - 1):
        inflight = [(work[tick], None) if tick < len(work) else None] + inflight[:-1]
        for k, fn in enumerate(stages):
            if inflight[k] is not None:
                item, st = inflight[k]
                inflight[k] = (item, fn(item, st))
    y_ref[0] = (x_ref[0]
                + jnp.dot(o_ref[...], wo_ref[:MIX, :], preferred_element_type=F32)
                + jnp.dot(xo_ref[0], wo_ref[MIX:, :], preferred_element_type=F32))


def _gqa_layer(x, nw, w_in, qn_t, kn_t, sink, cos, sin_signed, xqn_t, mk, mv, w_out):
    q, k2, v2, xo = _gqa_proj(x, nw, w_in, qn_t, kn_t, cos, sin_signed, xqn_t, mk, mv)
    return _win_attn(x, q, k2, v2, xo, sink, w_out)


def _gqa_proj(x, nw, w_in, qn_t, kn_t, cos, sin_signed, xqn_t, mk, mv):
    B, S, _ = x.shape
    tm = TM_PROJ
    tok = lambda w: pl.BlockSpec((1, tm, w), lambda b, j: (b, j, 0))
    full = lambda a: pl.BlockSpec(a.shape, lambda b, j: (0,) * a.ndim)
    memspec = pl.BlockSpec((1, 1, X_HEADS, N_MEM, X_W), lambda b, j: (1, b, 0, 0, 0))
    act = lambda w, dt: jax.ShapeDtypeStruct((B, S, w), dt)
    return pl.pallas_call(
        _gq_proj_kernel,
        grid=(B, S // tm),
        in_specs=[tok(D_MODEL), full(nw), full(w_in), full(qn_t), full(kn_t),
                  pl.BlockSpec((tm, LANES), lambda b, j: (j, 0)),
                  pl.BlockSpec((tm, LANES), lambda b, j: (j, 0)),
                  full(xqn_t), memspec, memspec],
        out_specs=[tok(MIX), tok(2 * KV_W), tok(2 * KV_W), tok(X_W)],
        out_shape=[act(MIX, BF16), act(2 * KV_W, BF16), act(2 * KV_W, BF16), act(X_W, BF16)],
        compiler_params=_params(2),
        name="gqa_proj",
    )(x, nw, w_in, qn_t, kn_t, cos, sin_signed, xqn_t, mk, mv)


def _win_attn(x, q, k2, v2, xo, sink, w_out):
    B, S, _ = x.shape
    ta = TM_ATT
    tok = lambda w, t: pl.BlockSpec((1, t, w), lambda b, j: (b, j, 0))
    full = lambda a: pl.BlockSpec(a.shape, lambda b, j: (0,) * a.ndim)
    act = lambda w, dt: jax.ShapeDtypeStruct((B, S, w), dt)
    r = ta // WINDOW
    last = S // WINDOW - 1
    prev = pl.BlockSpec((1, WINDOW, 2 * KV_W), lambda b, j: (b, jnp.maximum(j * r - 1, 0), 0))
    nxt = pl.BlockSpec((1, WINDOW, 2 * KV_W), lambda b, j: (b, jnp.minimum((j + 1) * r, last), 0))
    return pl.pallas_call(
        functools.partial(_win_attn_kernel, seq_len=S),
        grid=(B, S // ta),
        in_specs=[pl.BlockSpec(memory_space=pltpu.SMEM),
                  tok(D_MODEL, ta), tok(MIX, ta),
                  tok(2 * KV_W, ta), prev, nxt, tok(2 * KV_W, ta), prev, nxt,
                  tok(X_W, ta), full(w_out)],
        out_specs=tok(D_MODEL, ta),
        out_shape=act(D_MODEL, F32),
        scratch_shapes=[pltpu.VMEM((2 * N_KV, ta + 2 * WINDOW, LANES), BF16),
                        pltpu.VMEM((2 * N_KV, ta + 2 * WINDOW, LANES), BF16),
                        pltpu.VMEM((ta // WINDOW, WINDOW, 3 * WINDOW), F32),
                        pltpu.VMEM((ta, MIX), BF16)],
        compiler_params=_params(2),
        name="win_attn_out",
    )(sink, x, q, k2, k2, k2, v2, v2, v2, xo, w_out)


def _rope_tables(seq_len):
    lane = jnp.arange(LANES) % HD
    freqs = ROPE_THETA ** (-(2 * (lane % (HD // 2))).astype(F32) / HD)
    ang = jnp.arange(seq_len, dtype=F32)[:, None] * freqs[None, :]
    sign = jnp.where(lane < HD // 2, -1.0, 1.0)
    return jnp.cos(ang), jnp.sin(ang) * sign


def _trunk(x, mem, p):
    S = x.shape[1]
    mk, mv = _memkv(mem, p["norm_mem"], p["x_w_kv"], p["x_kn"])
    cos, sin_signed = _rope_tables(S)
    for i in range(DEPTH):
        if i % 2 == 0:
            a = i // 2
            assert a == 0 and i == 0, "the HGRN2 kernels read layer 0's forget-gate lower bounds"
            x = _hgrn2_layer(x, p["norm_mix"][i], p["hg_w_in"][a], p["hg_lb"], p["hg_gn"][a],
                             p["x_qn"][i], mk, mv, p["w_out"][i])
        else:
            b = i // 2
            assert i == 1, "the attention kernels read layer 1's memory keys and values"
            x = _gqa_layer(x, p["norm_mix"][i], p["gq_w_in"][b], p["gq_qn"][b], p["gq_kn"][b],
                           p["gq_sink"][b], cos, sin_signed, p["x_qn"][i], mk, mv, p["w_out"][i])
        x = _ffn(x, p["norm_ffn"][i], p["ffn_w_up"][i], p["ffn_conv_w"][i], p["ffn_conv_b"][i],
                 p["ffn_w_down"][i])
    return x


def _prepare(norm_mix, norm_mem, norm_ffn, hg_w_in, hg_lb, hg_gn, gq_w_in, gq_qn, gq_kn, gq_sink,
             x_w_kv, x_qn, x_kn, w_out, ffn_w_up, ffn_conv_w, ffn_conv_b, ffn_w_down):
    row = lambda a: a.reshape(a.shape[0], 1, a.shape[-1])
    return {
        "norm_mix": row(norm_mix), "norm_mem": row(norm_mem), "norm_ffn": row(norm_ffn),
        "hg_w_in": hg_w_in.astype(BF16), "hg_lb": hg_lb, "hg_gn": row(hg_gn),
        "gq_w_in": gq_w_in.astype(BF16),
        "gq_qn": row(jnp.tile(gq_qn, (1, N_Q))), "gq_kn": row(jnp.tile(gq_kn, (1, N_KV))),
        "gq_sink": gq_sink,
        "x_w_kv": x_w_kv.astype(BF16),
        "x_qn": row(jnp.tile(x_qn, (1, X_HEADS))), "x_kn": row(jnp.tile(x_kn, (1, X_HEADS))),
        "w_out": w_out.astype(BF16),
        "ffn_w_up": ffn_w_up.astype(BF16), "ffn_conv_w": ffn_conv_w, "ffn_conv_b": row(ffn_conv_b),
        "ffn_w_down": ffn_w_down.astype(BF16),
    }


def kernel(x_prompt, x_sample, mem_prompt, mem_sample, norm_mix, norm_mem, norm_ffn, hg_w_in, hg_lb, hg_gn, gq_w_in, gq_qn, gq_kn, gq_sink, x_w_kv, x_qn, x_kn, w_out, ffn_w_up, ffn_conv_w, ffn_conv_b, ffn_w_down):
    p = _prepare(norm_mix, norm_mem, norm_ffn, hg_w_in, hg_lb, hg_gn, gq_w_in, gq_qn, gq_kn,
                 gq_sink, x_w_kv, x_qn, x_kn, w_out, ffn_w_up, ffn_conv_w, ffn_conv_b, ffn_w_down)
    return (_trunk(x_prompt, mem_prompt, p), _trunk(x_sample, mem_sample, p))
```

```python
import functools

import jax
import jax.numpy as jnp
from jax import lax
from jax.experimental import pallas as pl
from jax.experimental.pallas import tpu as pltpu

F32 = jnp.float32
BF16 = jnp.bfloat16

D_MODEL = 1024
DEPTH = 2
MIX = 768
N_MEM = 256
X_HEADS = 4
X_HD = 64
X_W = X_HEADS * X_HD
HG_HEADS = 6
HG_DK = 128
HG_CHUNK = 64
N_Q = 12
N_KV = 4
HD = 64
GROUP = N_Q // N_KV
KV_W = N_KV * HD
WINDOW = 128
ROPE_THETA = 10000.0
D_FF = 2816
EPS = 1e-6

LANES = 128
MXU_N = 256
VMEM_LIMIT = 48 * 1024 * 1024

TM_HG = 512
HG_SUB = 256
TM_PROJ = 1024
PROJ_SUB = 256
TM_ATT = 512
ATT_GROUPS = 6
TM_FFN = 512
FFN_HALO = 8
FFN_TN = 256


def _params(n_axes):
    return pltpu.CompilerParams(
        dimension_semantics=("arbitrary",) * n_axes, vmem_limit_bytes=VMEM_LIMIT)


def _rms_rows(x, w):
    ms = jnp.mean(x * x, axis=-1, keepdims=True)
    return x * lax.rsqrt(ms + EPS) * w


def _head_ones(width, hd):
    r = lax.broadcasted_iota(jnp.int32, (width, width), 0) // hd
    c = lax.broadcasted_iota(jnp.int32, (width, width), 1) // hd
    return jnp.where(r == c, 1.0, 0.0).astype(BF16)


def _head_rms(x, hd):
    ones = _head_ones(MXU_N, hd)
    parts = []
    for g in range(x.shape[-1] // MXU_N):
        xs = x[:, g * MXU_N:(g + 1) * MXU_N]
        ss = jnp.dot((xs * xs).astype(BF16), ones, preferred_element_type=F32)
        parts.append(xs * lax.rsqrt(ss * (1.0 / hd) + EPS))
    return parts[0] if len(parts) == 1 else jnp.concatenate(parts, axis=-1)


def _dot_nt(a, b):
    return lax.dot_general(a, b, (((1,), (1,)), ((), ())), preferred_element_type=F32)


def _dot_tn(a, b):
    return lax.dot_general(a, b, (((0,), (0,)), ((), ())), preferred_element_type=F32)


def _round_robin(items, stages):
    inflight = [None] * len(stages)
    for tick in range(len(items) + len(stages) - 1):
        inflight = [(items[tick], None) if tick < len(items) else None] + inflight[:-1]
        for k, fn in enumerate(stages):
            if inflight[k] is not None:
                item, state = inflight[k]
                inflight[k] = (item, fn(item, state))


def _memkv_kernel(mem_ref, nw_ref, wkv_ref, kn_ref, mk_ref, mv_ref):
    mh = _rms_rows(mem_ref[0], nw_ref[0]).astype(BF16)
    kv = jnp.dot(mh, wkv_ref[0], preferred_element_type=F32)
    k = _head_rms(kv[:, :X_W], X_HD) * kn_ref[0]
    v = kv[:, X_W:]
    head = lax.broadcasted_iota(jnp.int32, (N_MEM, X_W), 1) // X_HD
    for h in range(X_HEADS):
        mk_ref[0, 0, h] = jnp.where(head == h, k, 0.0).astype(BF16)
        mv_ref[0, 0, h] = jnp.where(head == h, v, 0.0).astype(BF16)


def _memkv(mem, norm_mem, w_kv, kn_t):
    B = mem.shape[0]
    out = jax.ShapeDtypeStruct((DEPTH, B, X_HEADS, N_MEM, X_W), BF16)
    return pl.pallas_call(
        _memkv_kernel,
        grid=(DEPTH, B),
        in_specs=[
            pl.BlockSpec((1, N_MEM, D_MODEL), lambda l, b: (b, 0, 0)),
            pl.BlockSpec((1, 1, D_MODEL), lambda l, b: (l, 0, 0)),
            pl.BlockSpec((1, D_MODEL, 2 * X_W), lambda l, b: (l, 0, 0)),
            pl.BlockSpec((1, 1, X_W), lambda l, b: (l, 0, 0)),
        ],
        out_specs=[
            pl.BlockSpec((1, 1, X_HEADS, N_MEM, X_W), lambda l, b: (l, b, 0, 0, 0)),
            pl.BlockSpec((1, 1, X_HEADS, N_MEM, X_W), lambda l, b: (l, b, 0, 0, 0)),
        ],
        out_shape=[out, out],
        compiler_params=_params(2),
        name="mem_kv",
    )(mem, norm_mem, w_kv, kn_t)


def _mem_scores(xq, qn, mk_ref):
    q = ((_head_rms(xq, X_HD) * qn) * (X_HD ** -0.5)).astype(BF16)
    return [_dot_nt(q, mk_ref[h]) for h in range(X_HEADS)]


def _mem_out(scores, mv_ref):
    R = scores[0].shape[0]
    s = jnp.concatenate(scores, axis=0)
    p = jnp.exp(s - jnp.max(s, axis=-1, keepdims=True))
    pn = (p / jnp.sum(p, axis=-1, keepdims=True)).astype(BF16)
    acc = None
    for h in range(len(scores)):
        o = jnp.dot(pn[h * R:(h + 1) * R], mv_ref[h], preferred_element_type=F32)
        acc = o if acc is None else acc + o
    return acc


def _lower_bound(lb_ref, direction, layer):
    r = lb_ref[direction]
    e = jnp.exp(r - jnp.max(r, axis=0, keepdims=True))
    return jnp.sum(e[:layer + 1], axis=0, keepdims=True) / jnp.sum(e, axis=0, keepdims=True)


def _chunk_tri(n, reverse):
    r = lax.broadcasted_iota(jnp.int32, (n, n), 0)
    c = lax.broadcasted_iota(jnp.int32, (n, n), 1)
    same = (r // HG_CHUNK) == (c // HG_CHUNK)
    order = (c >= r) if reverse else (c <= r)
    return jnp.where(same & order, 1.0, 0.0).astype(BF16)


def _gla_prep(qs, f_raw, lb, qt_ref, kt_ref, reverse):
    R = qs.shape[0]
    f = lb + (1.0 - lb) * jax.nn.sigmoid(f_raw)
    g = jnp.log(f)
    k = 1.0 - f
    tri = _chunk_tri(R, reverse)
    g_hi = g.astype(BF16)
    g_lo = (g - g_hi.astype(F32)).astype(BF16)
    b = (jnp.dot(tri, g_hi, preferred_element_type=F32)
         + jnp.dot(tri, g_lo, preferred_element_type=F32))
    qt_ref[...] = (qs * jnp.exp(b)).astype(BF16)
    kt_ref[...] = (k * jnp.exp(-b)).astype(BF16)
    decays = []
    for c in range(R // HG_CHUNK):
        end_row = c * HG_CHUNK if reverse else (c + 1) * HG_CHUNK - 1
        decays.append(jnp.exp(b[end_row:end_row + 1, :]))
    return decays


def _gla_chunks(vb, decays, st_ref, qt_ref, kt_ref, o_ref, reverse):
    n_chunks = len(decays)
    ri = lax.broadcasted_iota(jnp.int32, (HG_CHUNK, HG_CHUNK), 0)
    ci = lax.broadcasted_iota(jnp.int32, (HG_CHUNK, HG_CHUNK), 1)
    keep = (ci >= ri) if reverse else (ci <= ri)
    order = list(range(n_chunks - 1, -1, -1) if reverse else range(n_chunks))

    def local(c):
        r0 = c * HG_CHUNK
        out = []
        for h in range(HG_HEADS):
            c0 = h * HG_DK
            qt = qt_ref[r0:r0 + HG_CHUNK, c0:c0 + HG_DK]
            kt = kt_ref[r0:r0 + HG_CHUNK, c0:c0 + HG_DK]
            vv = vb[r0:r0 + HG_CHUNK, c0:c0 + HG_DK]
            sc = jnp.where(keep, _dot_nt(qt, kt), 0.0).astype(BF16)
            out.append((qt, vv, sc, _dot_tn(vv, kt)))
        return out

    nxt = local(order[0])
    for n, c in enumerate(order):
        cur = nxt
        if n + 1 < n_chunks:
            nxt = local(order[n + 1])
        r0 = c * HG_CHUNK
        for h, (qt, vv, sc, kv) in enumerate(cur):
            c0 = h * HG_DK
            st = st_ref[c0:c0 + HG_DK, :]
            o_ref[r0:r0 + HG_CHUNK, c0:c0 + HG_DK] = (
                jnp.dot(sc, vv, preferred_element_type=F32) + _dot_nt(qt, st.astype(BF16)))
            st_ref[c0:c0 + HG_DK, :] = (st + kv) * decays[c][:, c0:c0 + HG_DK]


def _hg_fwd_kernel(x_ref, nw_ref, w_ref, lb_ref, qn_ref, mk_ref, mv_ref,
                   qs_ref, v_ref, fb_ref, sg_ref, of_ref, xo_ref,
                   st_ref, qt_ref, kt_ref):
    @pl.when(pl.program_id(1) == 0)
    def _():
        st_ref[...] = jnp.zeros_like(st_ref)

    lb = _lower_bound(lb_ref, 0, 0)
    rows = lambda i: slice(i * HG_SUB, (i + 1) * HG_SUB)

    def project(i, _):
        r = rows(i)
        h = _rms_rows(x_ref[0, r, :], nw_ref[...]).astype(BF16)
        proj = lambda c, w=MIX: jnp.dot(h, w_ref[:, c * MIX:c * MIX + w],
                                        preferred_element_type=F32)
        f_raw = proj(1)
        q = proj(0)
        cx = proj(5, X_W)
        vb = proj(3).astype(BF16)
        go = proj(4)
        fb_ref[0, r, :] = proj(2)
        qs = q * jax.nn.sigmoid(q)
        qs_ref[0, r, :] = qs.astype(BF16)
        v_ref[0, r, :] = vb
        sg_ref[0, r, :] = (go * jax.nn.sigmoid(go)).astype(BF16)
        return qs, f_raw, cx, vb

    def gates(i, state):
        qs, f_raw, cx, vb = state
        r = rows(i)
        mem_s = _mem_scores(cx, qn_ref[...], mk_ref.at[0, 0])
        decays = _gla_prep(qs, f_raw, lb, qt_ref.at[r], kt_ref.at[r], reverse=False)
        return mem_s, decays, vb

    def recur(i, state):
        mem_s, decays, vb = state
        r = rows(i)
        xo_ref[0, r, :] = _mem_out(mem_s, mv_ref.at[0, 0]).astype(BF16)
        _gla_chunks(vb, decays, st_ref, qt_ref.at[r], kt_ref.at[r], of_ref.at[0, r],
                    reverse=False)

    _round_robin(list(range(x_ref.shape[1] // HG_SUB)), [project, gates, recur])


def _hg_bwd_kernel(x_ref, qs_ref, v_ref, fb_ref, sg_ref, of_ref, xo_ref, lb_ref, gn_ref, wo_ref,
                   y_ref, st_ref, qt_ref, kt_ref, ob_ref):
    @pl.when(pl.program_id(1) == 0)
    def _():
        st_ref[...] = jnp.zeros_like(st_ref)

    lb = _lower_bound(lb_ref, 1, 0)
    rows = lambda i: slice(i * HG_SUB, (i + 1) * HG_SUB)

    def gates(i, _):
        r = rows(i)
        return _gla_prep(qs_ref[0, r, :].astype(F32), fb_ref[0, r, :], lb,
                         qt_ref.at[r], kt_ref.at[r], reverse=True)

    def recur(i, decays):
        r = rows(i)
        _gla_chunks(v_ref[0, r, :], decays, st_ref, qt_ref.at[r], kt_ref.at[r], ob_ref.at[r],
                    reverse=True)

    def output(i, _):
        r = rows(i)
        o = of_ref[0, r, :] + ob_ref[r, :]
        on = (_rms_rows(o, gn_ref[...]) * sg_ref[0, r, :].astype(F32)).astype(BF16)
        y_ref[0, r, :] = (x_ref[0, r, :]
                          + jnp.dot(on, wo_ref[:MIX, :], preferred_element_type=F32)
                          + jnp.dot(xo_ref[0, r, :], wo_ref[MIX:, :], preferred_element_type=F32))

    _round_robin(list(range(x_ref.shape[1] // HG_SUB - 1, -1, -1)), [gates, recur, output])


def _hgrn2_layer(x, nw, w_in, hg_lb, gn, qn_t, mk, mv, w_out):
    B, S, _ = x.shape
    tm = TM_HG
    nb = S // tm
    tok = lambda w: pl.BlockSpec((1, tm, w), lambda b, j: (b, j, 0))
    full = lambda a: pl.BlockSpec(a.shape, lambda b, j: (0,) * a.ndim)
    memspec = pl.BlockSpec((1, 1, X_HEADS, N_MEM, X_W), lambda b, j: (0, b, 0, 0, 0))
    act = lambda w, dt: jax.ShapeDtypeStruct((B, S, w), dt)
    qs, v, fb, sg, of, xo = pl.pallas_call(
        _hg_fwd_kernel,
        grid=(B, nb),
        in_specs=[tok(D_MODEL), full(nw), full(w_in), full(hg_lb), full(qn_t), memspec, memspec],
        out_specs=[tok(MIX), tok(MIX), tok(MIX), tok(MIX), tok(MIX), tok(X_W)],
        out_shape=[act(MIX, BF16), act(MIX, BF16), act(MIX, F32), act(MIX, BF16),
                   act(MIX, F32), act(X_W, BF16)],
        scratch_shapes=[pltpu.VMEM((MIX, HG_DK), F32), pltpu.VMEM((tm, MIX), BF16),
                        pltpu.VMEM((tm, MIX), BF16)],
        compiler_params=_params(2),
        name="hgrn2_fwd",
    )(x, nw, w_in, hg_lb, qn_t, mk, mv)

    rtok = lambda w: pl.BlockSpec((1, tm, w), lambda b, j: (b, nb - 1 - j, 0))
    return pl.pallas_call(
        _hg_bwd_kernel,
        grid=(B, nb),
        in_specs=[rtok(D_MODEL), rtok(MIX), rtok(MIX), rtok(MIX), rtok(MIX), rtok(MIX), rtok(X_W),
                  full(hg_lb), full(gn), full(w_out)],
        out_specs=rtok(D_MODEL),
        out_shape=act(D_MODEL, F32),
        scratch_shapes=[pltpu.VMEM((MIX, HG_DK), F32), pltpu.VMEM((tm, MIX), BF16),
                        pltpu.VMEM((tm, MIX), BF16), pltpu.VMEM((tm, MIX), F32)],
        compiler_params=_params(2),
        name="hgrn2_bwd_out",
    )(x, qs, v, fb, sg, of, xo, hg_lb, gn, w_out)


def _ffn_kernel(x_ref, xp_ref, xn_ref, nw_ref, wu_ref, cw_ref, cb_ref, wd_ref, y_ref,
                h_ref, u_ref, a_ref):
    j = pl.program_id(1)
    nb = pl.num_programs(1)
    tm = x_ref.shape[1]
    rows = tm + 2 * FFN_HALO
    nw = nw_ref[...]
    x = x_ref[0]
    halo = jnp.concatenate([xn_ref[0] * jnp.where(j < nb - 1, 1.0, 0.0),
                            xp_ref[0] * jnp.where(j > 0, 1.0, 0.0)], axis=0)
    h_ref[0:tm, :] = _rms_rows(x, nw).astype(BF16)
    h_ref[tm:, :] = _rms_rows(halo, nw).astype(BF16)
    hh = h_ref[...]

    def up(t):
        for half in range(2):
            c0 = half * D_FF + t * FFN_TN
            u_ref[t % 2, half] = jnp.dot(hh, wu_ref[:, c0:c0 + FFN_TN], preferred_element_type=F32)

    def conv(t, half):
        c0 = half * D_FF + t * FFN_TN
        u = u_ref[t % 2, half]
        w = cw_ref[:, c0:c0 + FFN_TN]
        return (pltpu.roll(u, 1, 0)[0:tm] * w[0:1] + u[0:tm] * w[1:2]
                + pltpu.roll(u, rows - 1, 0)[0:tm] * w[2:3] + cb_ref[:, c0:c0 + FFN_TN])

    n_tiles = D_FF // FFN_TN
    up(0)
    for t in range(n_tiles):
        if t + 1 < n_tiles:
            up(t + 1)
        gate = conv(t, 0)
        val = conv(t, 1)
        a_ref[:, t * FFN_TN:(t + 1) * FFN_TN] = (gate * jax.nn.sigmoid(gate) * val).astype(BF16)
    y_ref[0] = x + jnp.dot(a_ref[...], wd_ref[...], preferred_element_type=F32)


def _ffn(x, nw, w_up, conv_w, conv_b, w_down):
    B, S, _ = x.shape
    tm = TM_FFN
    nb = S // tm
    r = tm // FFN_HALO
    last = S // FFN_HALO - 1
    full = lambda a: pl.BlockSpec(a.shape, lambda b, j: (0,) * a.ndim)
    return pl.pallas_call(
        _ffn_kernel,
        grid=(B, nb),
        in_specs=[
            pl.BlockSpec((1, tm, D_MODEL), lambda b, j: (b, j, 0)),
            pl.BlockSpec((1, FFN_HALO, D_MODEL), lambda b, j: (b, jnp.maximum(j * r - 1, 0), 0)),
            pl.BlockSpec((1, FFN_HALO, D_MODEL),
                         lambda b, j: (b, jnp.minimum((j + 1) * r, last), 0)),
            full(nw), full(w_up), full(conv_w), full(conv_b), full(w_down),
        ],
        out_specs=pl.BlockSpec((1, tm, D_MODEL), lambda b, j: (b, j, 0)),
        out_shape=jax.ShapeDtypeStruct((B, S, D_MODEL), F32),
        scratch_shapes=[pltpu.VMEM((tm + 2 * FFN_HALO, D_MODEL), BF16),
                        pltpu.VMEM((2, 2, tm + 2 * FFN_HALO, FFN_TN), F32),
                        pltpu.VMEM((tm, D_FF), BF16)],
        compiler_params=_params(2),
        name="conv_glu",
    )(x, x, x, nw, w_up, conv_w, conv_b, w_down)


def _rope(x, cos, sin_signed, first_half):
    rot = jnp.where(first_half, pltpu.roll(x, LANES - HD // 2, 1), pltpu.roll(x, HD // 2, 1))
    return x * cos + rot * sin_signed


def _gq_proj_kernel(x_ref, nw_ref, w_ref, qn_ref, kn_ref, cos_ref, sin_ref, xqn_ref,
                    mk_ref, mv_ref, q_ref, k_ref, v_ref, xo_ref):
    rows = lambda i: slice(i * PROJ_SUB, (i + 1) * PROJ_SUB)

    def project(i, _):
        h = _rms_rows(x_ref[0, rows(i), :], nw_ref[...]).astype(BF16)
        return jnp.dot(h, w_ref[...], preferred_element_type=F32)

    def rotate(i, cols):
        r = rows(i)
        cos = cos_ref[r, :]
        sin = sin_ref[r, :]
        first_half = (lax.broadcasted_iota(jnp.int32, cos.shape, 1) % HD) < HD // 2
        qn = _head_rms(cols[:, :MIX], HD) * qn_ref[...]
        for g in range(MIX // LANES):
            sl = slice(g * LANES, (g + 1) * LANES)
            q_ref[0, r, sl] = (_rope(qn[:, sl], cos, sin, first_half) * (HD ** -0.5)).astype(BF16)
        kn = _head_rms(cols[:, MIX:MIX + KV_W], HD) * kn_ref[...]
        v = cols[:, MIX + KV_W:MIX + 2 * KV_W]
        for g in range(KV_W // LANES):
            sl = slice(g * LANES, (g + 1) * LANES)
            sw = slice(KV_W + g * LANES, KV_W + (g + 1) * LANES)
            kr = _rope(kn[:, sl], cos, sin, first_half)
            k_ref[0, r, sl] = kr.astype(BF16)
            k_ref[0, r, sw] = pltpu.roll(kr, HD, 1).astype(BF16)
            v_ref[0, r, sl] = v[:, sl].astype(BF16)
            v_ref[0, r, sw] = pltpu.roll(v[:, sl], HD, 1).astype(BF16)
        return _mem_scores(cols[:, MIX + 2 * KV_W:], xqn_ref[...], mk_ref.at[0, 0])

    def memory(i, mem_s):
        xo_ref[0, rows(i), :] = _mem_out(mem_s, mv_ref.at[0, 0]).astype(BF16)

    _round_robin(list(range(x_ref.shape[1] // PROJ_SUB)), [project, rotate, memory])


def _win_attn_kernel(sink_ref, x_ref, q_ref, k_ref, kp_ref, kn_ref, v_ref, vp_ref, vn_ref,
                     xo_ref, wo_ref, y_ref, kv_ref, vv_ref, bias_ref, o_ref, *, seq_len):
    j = pl.program_id(1)
    tm = x_ref.shape[1]
    rows = tm + 2 * WINDOW
    half = lax.broadcasted_iota(jnp.int32, (rows, LANES), 1) // HD
    for src, prev, nxt, dst in ((k_ref, kp_ref, kn_ref, kv_ref), (v_ref, vp_ref, vn_ref, vv_ref)):
        for kvh in range(N_KV):
            for hq in range(2):
                c0 = (kvh // 2) * LANES + (0 if kvh % 2 == hq else KV_W)
                cat = jnp.concatenate([prev[0, :, c0:c0 + LANES], src[0, :, c0:c0 + LANES],
                                       nxt[0, :, c0:c0 + LANES]], axis=0)
                dst[kvh * 2 + hq] = jnp.where(half == hq, cat, jnp.zeros_like(cat))

    qi = lax.broadcasted_iota(jnp.int32, (WINDOW, 3 * WINDOW), 0)
    kj = lax.broadcasted_iota(jnp.int32, (WINDOW, 3 * WINDOW), 1)
    n_sub = tm // WINDOW
    for i in range(n_sub):
        kpos = j * tm + (i - 1) * WINDOW + kj
        ok = (kj - qi >= 0) & (kj - qi <= 2 * WINDOW) & (kpos >= 0) & (kpos < seq_len)
        bias_ref[i] = jnp.where(ok, 0.0, -jnp.inf)

    slab = lambda g, hq: ((2 * g + hq) // GROUP) * 2 + hq

    def scores(i, gs):
        r0 = i * WINDOW
        out = []
        for g in gs:
            qp = q_ref[0, r0:r0 + WINDOW, g * LANES:(g + 1) * LANES]
            out += [_dot_nt(qp, kv_ref[slab(g, hq), r0:r0 + 3 * WINDOW, :]) for hq in range(2)]
        return out

    n_heads = 2 * ATT_GROUPS
    row_head = lax.broadcasted_iota(jnp.int32, (n_heads * WINDOW, 1), 0) // WINDOW
    groups = [tuple(range(g0, g0 + ATT_GROUPS)) for g0 in range(0, N_Q // 2, ATT_GROUPS)]
    work = [(i, gs) for i in range(n_sub) for gs in groups]

    def stage_max(item, s_raw):
        i, gs = item
        s = jnp.concatenate(s_raw, axis=0) + jnp.concatenate([bias_ref[i]] * n_heads, axis=0)
        sk = jnp.zeros((n_heads * WINDOW, 1), F32)
        for n, g in enumerate(gs):
            for hq in range(2):
                sk = jnp.where(row_head == 2 * n + hq, sink_ref[2 * g + hq], sk)
        return s, sk, jnp.maximum(jnp.max(s, axis=-1, keepdims=True), sk)

    def stage_exp(item, st):
        s, sk, m = st
        p = jnp.exp(s - m)
        return p, jnp.sum(p, axis=-1, keepdims=True) + jnp.exp(sk - m)

    def stage_out(item, st):
        i, gs = item
        p, den = st
        r0 = i * WINDOW
        pn = (p / den).astype(BF16)
        for n, g in enumerate(gs):
            acc = None
            for hq in range(2):
                rs = slice((2 * n + hq) * WINDOW, (2 * n + hq + 1) * WINDOW)
                o = jnp.dot(pn[rs], vv_ref[slab(g, hq), r0:r0 + 3 * WINDOW, :],
                            preferred_element_type=F32)
                acc = o if acc is None else acc + o
            o_ref[r0:r0 + WINDOW, g * LANES:(g + 1) * LANES] = acc.astype(BF16)

    _round_robin(work, [lambda item, _: scores(*item), stage_max, stage_exp, stage_out])
    y_ref[0] = (x_ref[0]
                + jnp.dot(o_ref[...], wo_ref[:MIX, :], preferred_element_type=F32)
                + jnp.dot(xo_ref[0], wo_ref[MIX:, :], preferred_element_type=F32))


def _gqa_layer(x, nw, w_in, qn_t, kn_t, sink, cos, sin_signed, xqn_t, mk, mv, w_out):
    q, k2, v2, xo = _gqa_proj(x, nw, w_in, qn_t, kn_t, cos, sin_signed, xqn_t, mk, mv)
    return _win_attn(x, q, k2, v2, xo, sink, w_out)


def _gqa_proj(x, nw, w_in, qn_t, kn_t, cos, sin_signed, xqn_t, mk, mv):
    B, S, _ = x.shape
    tm = TM_PROJ
    tok = lambda w: pl.BlockSpec((1, tm, w), lambda b, j: (b, j, 0))
    full = lambda a: pl.BlockSpec(a.shape, lambda b, j: (0,) * a.ndim)
    memspec = pl.BlockSpec((1, 1, X_HEADS, N_MEM, X_W), lambda b, j: (1, b, 0, 0, 0))
    act = lambda w, dt: jax.ShapeDtypeStruct((B, S, w), dt)
    return pl.pallas_call(
        _gq_proj_kernel,
        grid=(B, S // tm),
        in_specs=[tok(D_MODEL), full(nw), full(w_in), full(qn_t), full(kn_t),
                  pl.BlockSpec((tm, LANES), lambda b, j: (j, 0)),
                  pl.BlockSpec((tm, LANES), lambda b, j: (j, 0)),
                  full(xqn_t), memspec, memspec],
        out_specs=[tok(MIX), tok(2 * KV_W), tok(2 * KV_W), tok(X_W)],
        out_shape=[act(MIX, BF16), act(2 * KV_W, BF16), act(2 * KV_W, BF16), act(X_W, BF16)],
        compiler_params=_params(2),
        name="gqa_proj",
    )(x, nw, w_in, qn_t, kn_t, cos, sin_signed, xqn_t, mk, mv)


def _win_attn(x, q, k2, v2, xo, sink, w_out):
    B, S, _ = x.shape
    ta = TM_ATT
    tok = lambda w, t: pl.BlockSpec((1, t, w), lambda b, j: (b, j, 0))
    full = lambda a: pl.BlockSpec(a.shape, lambda b, j: (0,) * a.ndim)
    act = lambda w, dt: jax.ShapeDtypeStruct((B, S, w), dt)
    r = ta // WINDOW
    last = S // WINDOW - 1
    prev = pl.BlockSpec((1, WINDOW, 2 * KV_W), lambda b, j: (b, jnp.maximum(j * r - 1, 0), 0))
    nxt = pl.BlockSpec((1, WINDOW, 2 * KV_W), lambda b, j: (b, jnp.minimum((j + 1) * r, last), 0))
    return pl.pallas_call(
        functools.partial(_win_attn_kernel, seq_len=S),
        grid=(B, S // ta),
        in_specs=[pl.BlockSpec(memory_space=pltpu.SMEM),
                  tok(D_MODEL, ta), tok(MIX, ta),
                  tok(2 * KV_W, ta), prev, nxt, tok(2 * KV_W, ta), prev, nxt,
                  tok(X_W, ta), full(w_out)],
        out_specs=tok(D_MODEL, ta),
        out_shape=act(D_MODEL, F32),
        scratch_shapes=[pltpu.VMEM((2 * N_KV, ta + 2 * WINDOW, LANES), BF16),
                        pltpu.VMEM((2 * N_KV, ta + 2 * WINDOW, LANES), BF16),
                        pltpu.VMEM((ta // WINDOW, WINDOW, 3 * WINDOW), F32),
                        pltpu.VMEM((ta, MIX), BF16)],
        compiler_params=_params(2),
        name="win_attn_out",
    )(sink, x, q, k2, k2, k2, v2, v2, v2, xo, w_out)


def _rope_tables(seq_len):
    lane = jnp.arange(LANES) % HD
    freqs = ROPE_THETA ** (-(2 * (lane % (HD // 2))).astype(F32) / HD)
    ang = jnp.arange(seq_len, dtype=F32)[:, None] * freqs[None, :]
    sign = jnp.where(lane < HD // 2, -1.0, 1.0)
    return jnp.cos(ang), jnp.sin(ang) * sign


def _trunk(x, mem, p):
    S = x.shape[1]
    mk, mv = _memkv(mem, p["norm_mem"], p["x_w_kv"], p["x_kn"])
    cos, sin_signed = _rope_tables(S)
    for i in range(DEPTH):
        if i % 2 == 0:
            a = i // 2
            assert a == 0 and i == 0, "the HGRN2 kernels read layer 0's forget-gate lower bounds"
            x = _hgrn2_layer(x, p["norm_mix"][i], p["hg_w_in"][a], p["hg_lb"], p["hg_gn"][a],
                             p["x_qn"][i], mk, mv, p["w_out"][i])
        else:
            b = i // 2
            assert i == 1, "the attention kernels read layer 1's memory keys and values"
            x = _gqa_layer(x, p["norm_mix"][i], p["gq_w_in"][b], p["gq_qn"][b], p["gq_kn"][b],
                           p["gq_sink"][b], cos, sin_signed, p["x_qn"][i], mk, mv, p["w_out"][i])
        x = _ffn(x, p["norm_ffn"][i], p["ffn_w_up"][i], p["ffn_conv_w"][i], p["ffn_conv_b"][i],
                 p["ffn_w_down"][i])
    return x


def _prepare(norm_mix, norm_mem, norm_ffn, hg_w_in, hg_lb, hg_gn, gq_w_in, gq_qn, gq_kn, gq_sink,
             x_w_kv, x_qn, x_kn, w_out, ffn_w_up, ffn_conv_w, ffn_conv_b, ffn_w_down):
    row = lambda a: a.reshape(a.shape[0], 1, a.shape[-1])
    return {
        "norm_mix": row(norm_mix), "norm_mem": row(norm_mem), "norm_ffn": row(norm_ffn),
        "hg_w_in": hg_w_in.astype(BF16), "hg_lb": hg_lb, "hg_gn": row(hg_gn),
        "gq_w_in": gq_w_in.astype(BF16),
        "gq_qn": row(jnp.tile(gq_qn, (1, N_Q))), "gq_kn": row(jnp.tile(gq_kn, (1, N_KV))),
        "gq_sink": gq_sink,
        "x_w_kv": x_w_kv.astype(BF16),
        "x_qn": row(jnp.tile(x_qn, (1, X_HEADS))), "x_kn": row(jnp.tile(x_kn, (1, X_HEADS))),
        "w_out": w_out.astype(BF16),
        "ffn_w_up": ffn_w_up.astype(BF16), "ffn_conv_w": ffn_conv_w, "ffn_conv_b": row(ffn_conv_b),
        "ffn_w_down": ffn_w_down.astype(BF16),
    }


def kernel(x_prompt, x_sample, mem_prompt, mem_sample, norm_mix, norm_mem, norm_ffn, hg_w_in, hg_lb, hg_gn, gq_w_in, gq_qn, gq_kn, gq_sink, x_w_kv, x_qn, x_kn, w_out, ffn_w_up, ffn_conv_w, ffn_conv_b, ffn_w_down):
    p = _prepare(norm_mix, norm_mem, norm_ffn, hg_w_in, hg_lb, hg_gn, gq_w_in, gq_qn, gq_kn,
                 gq_sink, x_w_kv, x_qn, x_kn, w_out, ffn_w_up, ffn_conv_w, ffn_conv_b, ffn_w_down)
    return (_trunk(x_prompt, mem_prompt, p), _trunk(x_sample, mem_sample, p))
```

```python
import functools

import jax
import jax.numpy as jnp
from jax import lax
from jax.experimental import pallas as pl
from jax.experimental.pallas import tpu as pltpu

F32 = jnp.float32
BF16 = jnp.bfloat16

D_MODEL = 1024
DEPTH = 2
MIX = 768
N_MEM = 256
X_HEADS = 4
X_HD = 64
X_W = X_HEADS * X_HD
HG_HEADS = 6
HG_DK = 128
HG_CHUNK = 64
N_Q = 12
N_KV = 4
HD = 64
GROUP = N_Q // N_KV
KV_W = N_KV * HD
WINDOW = 128
ROPE_THETA = 10000.0
D_FF = 2816
EPS = 1e-6

LANES = 128
MXU_N = 256
VMEM_LIMIT = 56 * 1024 * 1024

TM_HG = 1024
HG_SUB = 256
TM_PROJ = 1024
PROJ_SUB = 256
TM_ATT = 512
ATT_GROUPS = 6
TM_FFN = 1024
FFN_HALO = 8
FFN_TN = 256


def _params(n_axes):
    return pltpu.CompilerParams(
        dimension_semantics=("arbitrary",) * n_axes, vmem_limit_bytes=VMEM_LIMIT)


def _rms_rows(x, w):
    ms = jnp.mean(x * x, axis=-1, keepdims=True)
    return x * lax.rsqrt(ms + EPS) * w


def _head_ones(width, hd):
    r = lax.broadcasted_iota(jnp.int32, (width, width), 0) // hd
    c = lax.broadcasted_iota(jnp.int32, (width, width), 1) // hd
    return jnp.where(r == c, 1.0, 0.0).astype(BF16)


def _head_rms(x, hd):
    ones = _head_ones(MXU_N, hd)
    parts = []
    for g in range(x.shape[-1] // MXU_N):
        xs = x[:, g * MXU_N:(g + 1) * MXU_N]
        ss = jnp.dot((xs * xs).astype(BF16), ones, preferred_element_type=F32)
        parts.append(xs * lax.rsqrt(ss * (1.0 / hd) + EPS))
    return parts[0] if len(parts) == 1 else jnp.concatenate(parts, axis=-1)


def _dot_nt(a, b):
    return lax.dot_general(a, b, (((1,), (1,)), ((), ())), preferred_element_type=F32)


def _dot_tn(a, b):
    return lax.dot_general(a, b, (((0,), (0,)), ((), ())), preferred_element_type=F32)


def _round_robin(items, stages):
    inflight = [None] * len(stages)
    for tick in range(len(items) + len(stages) - 1):
        inflight = [(items[tick], None) if tick < len(items) else None] + inflight[:-1]
        for k, fn in enumerate(stages):
            if inflight[k] is not None:
                item, state = inflight[k]
                inflight[k] = (item, fn(item, state))


def _memkv_kernel(mem_ref, nw_ref, wkv_ref, kn_ref, mk_ref, mv_ref):
    mh = _rms_rows(mem_ref[0], nw_ref[0]).astype(BF16)
    kv = jnp.dot(mh, wkv_ref[0], preferred_element_type=F32)
    k = _head_rms(kv[:, :X_W], X_HD) * kn_ref[0]
    v = kv[:, X_W:]
    head = lax.broadcasted_iota(jnp.int32, (N_MEM, X_W), 1) // X_HD
    for h in range(X_HEADS):
        mk_ref[0, 0, h] = jnp.where(head == h, k, 0.0).astype(BF16)
        mv_ref[0, 0, h] = jnp.where(head == h, v, 0.0).astype(BF16)


def _memkv(mem, norm_mem, w_kv, kn_t):
    B = mem.shape[0]
    out = jax.ShapeDtypeStruct((DEPTH, B, X_HEADS, N_MEM, X_W), BF16)
    return pl.pallas_call(
        _memkv_kernel,
        grid=(DEPTH, B),
        in_specs=[
            pl.BlockSpec((1, N_MEM, D_MODEL), lambda l, b: (b, 0, 0)),
            pl.BlockSpec((1, 1, D_MODEL), lambda l, b: (l, 0, 0)),
            pl.BlockSpec((1, D_MODEL, 2 * X_W), lambda l, b: (l, 0, 0)),
            pl.BlockSpec((1, 1, X_W), lambda l, b: (l, 0, 0)),
        ],
        out_specs=[
            pl.BlockSpec((1, 1, X_HEADS, N_MEM, X_W), lambda l, b: (l, b, 0, 0, 0)),
            pl.BlockSpec((1, 1, X_HEADS, N_MEM, X_W), lambda l, b: (l, b, 0, 0, 0)),
        ],
        out_shape=[out, out],
        compiler_params=_params(2),
        name="mem_kv",
    )(mem, norm_mem, w_kv, kn_t)


def _mem_scores(xq, qn, mk_ref):
    q = ((_head_rms(xq, X_HD) * qn) * (X_HD ** -0.5)).astype(BF16)
    return [_dot_nt(q, mk_ref[h]) for h in range(X_HEADS)]


def _mem_out(scores, mv_ref):
    R = scores[0].shape[0]
    s = jnp.concatenate(scores, axis=0)
    p = jnp.exp(s - jnp.max(s, axis=-1, keepdims=True))
    pn = (p / jnp.sum(p, axis=-1, keepdims=True)).astype(BF16)
    acc = None
    for h in range(len(scores)):
        o = jnp.dot(pn[h * R:(h + 1) * R], mv_ref[h], preferred_element_type=F32)
        acc = o if acc is None else acc + o
    return acc


def _lower_bound(lb_ref, direction, layer):
    r = lb_ref[direction]
    e = jnp.exp(r - jnp.max(r, axis=0, keepdims=True))
    return jnp.sum(e[:layer + 1], axis=0, keepdims=True) / jnp.sum(e, axis=0, keepdims=True)


def _chunk_tri(n, reverse):
    r = lax.broadcasted_iota(jnp.int32, (n, n), 0)
    c = lax.broadcasted_iota(jnp.int32, (n, n), 1)
    same = (r // HG_CHUNK) == (c // HG_CHUNK)
    order = (c >= r) if reverse else (c <= r)
    return jnp.where(same & order, 1.0, 0.0).astype(BF16)


def _gla_prep(qs, f_raw, lb, qt_ref, kt_ref, reverse):
    R = qs.shape[0]
    f = lb + (1.0 - lb) * jax.nn.sigmoid(f_raw)
    g = jnp.log(f)
    k = 1.0 - f
    tri = _chunk_tri(R, reverse)
    g_hi = g.astype(BF16)
    g_lo = (g - g_hi.astype(F32)).astype(BF16)
    b = (jnp.dot(tri, g_hi, preferred_element_type=F32)
         + jnp.dot(tri, g_lo, preferred_element_type=F32))
    qt_ref[...] = (qs * jnp.exp(b)).astype(BF16)
    kt_ref[...] = (k * jnp.exp(-b)).astype(BF16)
    decays = []
    for c in range(R // HG_CHUNK):
        end_row = c * HG_CHUNK if reverse else (c + 1) * HG_CHUNK - 1
        decays.append(jnp.exp(b[end_row:end_row + 1, :]))
    return decays


def _gla_chunks(vb, decays, st_ref, qt_ref, kt_ref, o_ref, reverse):
    n_chunks = len(decays)
    ri = lax.broadcasted_iota(jnp.int32, (HG_CHUNK, HG_CHUNK), 0)
    ci = lax.broadcasted_iota(jnp.int32, (HG_CHUNK, HG_CHUNK), 1)
    keep = (ci >= ri) if reverse else (ci <= ri)
    order = list(range(n_chunks - 1, -1, -1) if reverse else range(n_chunks))

    def local(c):
        r0 = c * HG_CHUNK
        out = []
        for h in range(HG_HEADS):
            c0 = h * HG_DK
            qt = qt_ref[r0:r0 + HG_CHUNK, c0:c0 + HG_DK]
            kt = kt_ref[r0:r0 + HG_CHUNK, c0:c0 + HG_DK]
            vv = vb[r0:r0 + HG_CHUNK, c0:c0 + HG_DK]
            sc = jnp.where(keep, _dot_nt(qt, kt), 0.0).astype(BF16)
            out.append((qt, vv, sc, _dot_tn(vv, kt)))
        return out

    nxt = local(order[0])
    for n, c in enumerate(order):
        cur = nxt
        if n + 1 < n_chunks:
            nxt = local(order[n + 1])
        r0 = c * HG_CHUNK
        for h, (qt, vv, sc, kv) in enumerate(cur):
            c0 = h * HG_DK
            st = st_ref[c0:c0 + HG_DK, :]
            o_ref[r0:r0 + HG_CHUNK, c0:c0 + HG_DK] = (
                jnp.dot(sc, vv, preferred_element_type=F32) + _dot_nt(qt, st.astype(BF16)))
            st_ref[c0:c0 + HG_DK, :] = (st + kv) * decays[c][:, c0:c0 + HG_DK]


def _hg_fwd_kernel(x_ref, nw_ref, w_ref, lb_ref, qn_ref, mk_ref, mv_ref,
                   qs_ref, v_ref, fb_ref, sg_ref, of_ref, xo_ref,
                   st_ref, qt_ref, kt_ref):
    @pl.when(pl.program_id(1) == 0)
    def _():
        st_ref[...] = jnp.zeros_like(st_ref)

    lb = _lower_bound(lb_ref, 0, 0)
    rows = lambda i: slice(i * HG_SUB, (i + 1) * HG_SUB)

    def project(i, _):
        r = rows(i)
        h = _rms_rows(x_ref[0, r, :], nw_ref[...]).astype(BF16)
        proj = lambda c, w=MIX: jnp.dot(h, w_ref[:, c * MIX:c * MIX + w],
                                        preferred_element_type=F32)
        f_raw = proj(1)
        q = proj(0)
        cx = proj(5, X_W)
        vb = proj(3).astype(BF16)
        go = proj(4)
        fb_ref[0, r, :] = proj(2)
        qs = q * jax.nn.sigmoid(q)
        qs_ref[0, r, :] = qs.astype(BF16)
        v_ref[0, r, :] = vb
        sg_ref[0, r, :] = (go * jax.nn.sigmoid(go)).astype(BF16)
        return qs, f_raw, cx, vb

    def gates(i, state):
        qs, f_raw, cx, vb = state
        r = rows(i)
        mem_s = _mem_scores(cx, qn_ref[...], mk_ref.at[0, 0])
        decays = _gla_prep(qs, f_raw, lb, qt_ref.at[r], kt_ref.at[r], reverse=False)
        return mem_s, decays, vb

    def recur(i, state):
        mem_s, decays, vb = state
        r = rows(i)
        xo_ref[0, r, :] = _mem_out(mem_s, mv_ref.at[0, 0]).astype(BF16)
        _gla_chunks(vb, decays, st_ref, qt_ref.at[r], kt_ref.at[r], of_ref.at[0, r],
                    reverse=False)

    _round_robin(list(range(x_ref.shape[1] // HG_SUB)), [project, gates, recur])


def _hg_bwd_kernel(x_ref, qs_ref, v_ref, fb_ref, sg_ref, of_ref, xo_ref, lb_ref, gn_ref, wo_ref,
                   y_ref, st_ref, qt_ref, kt_ref, ob_ref):
    @pl.when(pl.program_id(1) == 0)
    def _():
        st_ref[...] = jnp.zeros_like(st_ref)

    lb = _lower_bound(lb_ref, 1, 0)
    rows = lambda i: slice(i * HG_SUB, (i + 1) * HG_SUB)

    def gates(i, _):
        r = rows(i)
        return _gla_prep(qs_ref[0, r, :].astype(F32), fb_ref[0, r, :], lb,
                         qt_ref.at[r], kt_ref.at[r], reverse=True)

    def recur(i, decays):
        r = rows(i)
        _gla_chunks(v_ref[0, r, :], decays, st_ref, qt_ref.at[r], kt_ref.at[r], ob_ref.at[r],
                    reverse=True)

    def output(i, _):
        r = rows(i)
        o = of_ref[0, r, :] + ob_ref[r, :]
        on = (_rms_rows(o, gn_ref[...]) * sg_ref[0, r, :].astype(F32)).astype(BF16)
        y_ref[0, r, :] = (x_ref[0, r, :]
                          + jnp.dot(on, wo_ref[:MIX, :], preferred_element_type=F32)
                          + jnp.dot(xo_ref[0, r, :], wo_ref[MIX:, :], preferred_element_type=F32))

    _round_robin(list(range(x_ref.shape[1] // HG_SUB - 1, -1, -1)), [gates, recur, output])


def _layer_spec(a, layer):
    return pl.BlockSpec((None,) + a.shape[1:], lambda b, j: (layer,) + (0,) * (a.ndim - 1),
                        pipeline_mode=pl.Buffered(1))


def _hgrn2_layer(x, layer, nw, w_in, hg_lb, gn, qn_t, mk, mv, w_out):
    B, S, _ = x.shape
    tm = TM_HG
    nb = S // tm
    tok = lambda w: pl.BlockSpec((1, tm, w), lambda b, j: (b, j, 0))
    full = lambda a: pl.BlockSpec(a.shape, lambda b, j: (0,) * a.ndim)
    memspec = pl.BlockSpec((1, 1, X_HEADS, N_MEM, X_W), lambda b, j: (0, b, 0, 0, 0))
    act = lambda w, dt: jax.ShapeDtypeStruct((B, S, w), dt)
    qs, v, fb, sg, of, xo = pl.pallas_call(
        _hg_fwd_kernel,
        grid=(B, nb),
        in_specs=[tok(D_MODEL), full(nw), full(w_in), full(hg_lb), full(qn_t), memspec, memspec],
        out_specs=[tok(MIX), tok(MIX), tok(MIX), tok(MIX), tok(MIX), tok(X_W)],
        out_shape=[act(MIX, BF16), act(MIX, BF16), act(MIX, F32), act(MIX, BF16),
                   act(MIX, F32), act(X_W, BF16)],
        scratch_shapes=[pltpu.VMEM((MIX, HG_DK), F32), pltpu.VMEM((tm, MIX), BF16),
                        pltpu.VMEM((tm, MIX), BF16)],
        compiler_params=_params(2),
        name="hgrn2_fwd",
    )(x, nw, w_in, hg_lb, qn_t, mk, mv)

    rtok = lambda w: pl.BlockSpec((1, tm, w), lambda b, j: (b, nb - 1 - j, 0))
    return pl.pallas_call(
        _hg_bwd_kernel,
        grid=(B, nb),
        in_specs=[rtok(D_MODEL), rtok(MIX), rtok(MIX), rtok(MIX), rtok(MIX), rtok(MIX), rtok(X_W),
                  full(hg_lb), full(gn), _layer_spec(w_out, layer)],
        out_specs=rtok(D_MODEL),
        out_shape=act(D_MODEL, F32),
        scratch_shapes=[pltpu.VMEM((MIX, HG_DK), F32), pltpu.VMEM((tm, MIX), BF16),
                        pltpu.VMEM((tm, MIX), BF16), pltpu.VMEM((tm, MIX), F32)],
        compiler_params=_params(2),
        name="hgrn2_bwd_out",
    )(x, qs, v, fb, sg, of, xo, hg_lb, gn, w_out)


def _ffn_kernel(x_ref, xp_ref, xn_ref, nw_ref, wu_ref, cw_ref, cb_ref, wd_ref, y_ref,
                h_ref, u_ref, a_ref):
    j = pl.program_id(1)
    nb = pl.num_programs(1)
    tm = x_ref.shape[1]
    rows = tm + 2 * FFN_HALO
    nw = nw_ref[...]
    x = x_ref[0]
    halo = jnp.concatenate([xn_ref[0] * jnp.where(j < nb - 1, 1.0, 0.0),
                            xp_ref[0] * jnp.where(j > 0, 1.0, 0.0)], axis=0)
    h_ref[0:tm, :] = _rms_rows(x, nw).astype(BF16)
    h_ref[tm:, :] = _rms_rows(halo, nw).astype(BF16)
    hh = h_ref[...]

    def up(t):
        for half in range(2):
            c0 = half * D_FF + t * FFN_TN
            u_ref[t % 2, half] = jnp.dot(hh, wu_ref[:, c0:c0 + FFN_TN], preferred_element_type=F32)

    def conv(t, half):
        c0 = half * D_FF + t * FFN_TN
        u = u_ref[t % 2, half]
        w = cw_ref[:, c0:c0 + FFN_TN]
        return (pltpu.roll(u, 1, 0)[0:tm] * w[0:1] + u[0:tm] * w[1:2]
                + pltpu.roll(u, rows - 1, 0)[0:tm] * w[2:3] + cb_ref[:, c0:c0 + FFN_TN])

    n_tiles = D_FF // FFN_TN
    up(0)
    for t in range(n_tiles):
        if t + 1 < n_tiles:
            up(t + 1)
        gate = conv(t, 0)
        val = conv(t, 1)
        a_ref[:, t * FFN_TN:(t + 1) * FFN_TN] = (gate * jax.nn.sigmoid(gate) * val).astype(BF16)
    y_ref[0] = x + jnp.dot(a_ref[...], wd_ref[...], preferred_element_type=F32)


def _ffn(x, layer, nw, w_up, conv_w, conv_b, w_down):
    B, S, _ = x.shape
    tm = TM_FFN
    nb = S // tm
    r = tm // FFN_HALO
    last = S // FFN_HALO - 1
    full = lambda a: _layer_spec(a, layer)
    return pl.pallas_call(
        _ffn_kernel,
        grid=(B, nb),
        in_specs=[
            pl.BlockSpec((1, tm, D_MODEL), lambda b, j: (b, j, 0)),
            pl.BlockSpec((1, FFN_HALO, D_MODEL), lambda b, j: (b, jnp.maximum(j * r - 1, 0), 0)),
            pl.BlockSpec((1, FFN_HALO, D_MODEL),
                         lambda b, j: (b, jnp.minimum((j + 1) * r, last), 0)),
            full(nw), full(w_up), full(conv_w), full(conv_b), full(w_down),
        ],
        out_specs=pl.BlockSpec((1, tm, D_MODEL), lambda b, j: (b, j, 0)),
        out_shape=jax.ShapeDtypeStruct((B, S, D_MODEL), F32),
        scratch_shapes=[pltpu.VMEM((tm + 2 * FFN_HALO, D_MODEL), BF16),
                        pltpu.VMEM((2, 2, tm + 2 * FFN_HALO, FFN_TN), F32),
                        pltpu.VMEM((tm, D_FF), BF16)],
        compiler_params=_params(2),
        name="conv_glu",
    )(x, x, x, nw, w_up, conv_w, conv_b, w_down)


def _rope(x, cos, sin_signed, first_half):
    rot = jnp.where(first_half, pltpu.roll(x, LANES - HD // 2, 1), pltpu.roll(x, HD // 2, 1))
    return x * cos + rot * sin_signed


def _gq_proj_kernel(x_ref, nw_ref, w_ref, qn_ref, kn_ref, cos_ref, sin_ref, xqn_ref,
                    mk_ref, mv_ref, q_ref, k_ref, v_ref, xo_ref):
    rows = lambda i: slice(i * PROJ_SUB, (i + 1) * PROJ_SUB)

    def project(i, _):
        h = _rms_rows(x_ref[0, rows(i), :], nw_ref[...]).astype(BF16)
        return jnp.dot(h, w_ref[...], preferred_element_type=F32)

    def rotate(i, cols):
        r = rows(i)
        cos = cos_ref[r, :]
        sin = sin_ref[r, :]
        first_half = (lax.broadcasted_iota(jnp.int32, cos.shape, 1) % HD) < HD // 2
        qn = _head_rms(cols[:, :MIX], HD) * qn_ref[...]
        for g in range(MIX // LANES):
            sl = slice(g * LANES, (g + 1) * LANES)
            q_ref[0, r, sl] = (_rope(qn[:, sl], cos, sin, first_half) * (HD ** -0.5)).astype(BF16)
        kn = _head_rms(cols[:, MIX:MIX + KV_W], HD) * kn_ref[...]
        v = cols[:, MIX + KV_W:MIX + 2 * KV_W]
        for g in range(KV_W // LANES):
            sl = slice(g * LANES, (g + 1) * LANES)
            sw = slice(KV_W + g * LANES, KV_W + (g + 1) * LANES)
            kr = _rope(kn[:, sl], cos, sin, first_half)
            k_ref[0, r, sl] = kr.astype(BF16)
            k_ref[0, r, sw] = pltpu.roll(kr, HD, 1).astype(BF16)
            v_ref[0, r, sl] = v[:, sl].astype(BF16)
            v_ref[0, r, sw] = pltpu.roll(v[:, sl], HD, 1).astype(BF16)
        return _mem_scores(cols[:, MIX + 2 * KV_W:], xqn_ref[...], mk_ref.at[0, 0])

    def memory(i, mem_s):
        xo_ref[0, rows(i), :] = _mem_out(mem_s, mv_ref.at[0, 0]).astype(BF16)

    _round_robin(list(range(x_ref.shape[1] // PROJ_SUB)), [project, rotate, memory])


def _win_attn_kernel(sink_ref, x_ref, q_ref, k_ref, kp_ref, kn_ref, v_ref, vp_ref, vn_ref,
                     xo_ref, wo_ref, y_ref, kv_ref, vv_ref, bias_ref, o_ref, *, seq_len):
    j = pl.program_id(1)
    tm = x_ref.shape[1]
    rows = tm + 2 * WINDOW
    half = lax.broadcasted_iota(jnp.int32, (rows, LANES), 1) // HD
    for src, prev, nxt, dst in ((k_ref, kp_ref, kn_ref, kv_ref), (v_ref, vp_ref, vn_ref, vv_ref)):
        for kvh in range(N_KV):
            for hq in range(2):
                c0 = (kvh // 2) * LANES + (0 if kvh % 2 == hq else KV_W)
                cat = jnp.concatenate([prev[0, :, c0:c0 + LANES], src[0, :, c0:c0 + LANES],
                                       nxt[0, :, c0:c0 + LANES]], axis=0)
                dst[kvh * 2 + hq] = jnp.where(half == hq, cat, jnp.zeros_like(cat))

    qi = lax.broadcasted_iota(jnp.int32, (WINDOW, 3 * WINDOW), 0)
    kj = lax.broadcasted_iota(jnp.int32, (WINDOW, 3 * WINDOW), 1)
    n_sub = tm // WINDOW
    for i in range(n_sub):
        kpos = j * tm + (i - 1) * WINDOW + kj
        ok = (kj - qi >= 0) & (kj - qi <= 2 * WINDOW) & (kpos >= 0) & (kpos < seq_len)
        bias_ref[i] = jnp.where(ok, 0.0, -jnp.inf)

    slab = lambda g, hq: ((2 * g + hq) // GROUP) * 2 + hq

    def scores(i, gs):
        r0 = i * WINDOW
        out = []
        for g in gs:
            qp = q_ref[0, r0:r0 + WINDOW, g * LANES:(g + 1) * LANES]
            out += [_dot_nt(qp, kv_ref[slab(g, hq), r0:r0 + 3 * WINDOW, :]) for hq in range(2)]
        return out

    n_heads = 2 * ATT_GROUPS
    row_head = lax.broadcasted_iota(jnp.int32, (n_heads * WINDOW, 1), 0) // WINDOW
    groups = [tuple(range(g0, g0 + ATT_GROUPS)) for g0 in range(0, N_Q // 2, ATT_GROUPS)]
    work = [(i, gs) for i in range(n_sub) for gs in groups]

    def stage_max(item, s_raw):
        i, gs = item
        s = jnp.concatenate(s_raw, axis=0) + jnp.concatenate([bias_ref[i]] * n_heads, axis=0)
        sk = jnp.zeros((n_heads * WINDOW, 1), F32)
        for n, g in enumerate(gs):
            for hq in range(2):
                sk = jnp.where(row_head == 2 * n + hq, sink_ref[2 * g + hq], sk)
        return s, sk, jnp.maximum(jnp.max(s, axis=-1, keepdims=True), sk)

    def stage_exp(item, st):
        s, sk, m = st
        p = jnp.exp(s - m)
        return p, jnp.sum(p, axis=-1, keepdims=True) + jnp.exp(sk - m)

    def stage_out(item, st):
        i, gs = item
        p, den = st
        r0 = i * WINDOW
        pn = (p / den).astype(BF16)
        for n, g in enumerate(gs):
            acc = None
            for hq in range(2):
                rs = slice((2 * n + hq) * WINDOW, (2 * n + hq + 1) * WINDOW)
                o = jnp.dot(pn[rs], vv_ref[slab(g, hq), r0:r0 + 3 * WINDOW, :],
                            preferred_element_type=F32)
                acc = o if acc is None else acc + o
            o_ref[r0:r0 + WINDOW, g * LANES:(g + 1) * LANES] = acc.astype(BF16)

    _round_robin(work, [lambda item, _: scores(*item), stage_max, stage_exp, stage_out])
    y_ref[0] = (x_ref[0]
                + jnp.dot(o_ref[...], wo_ref[:MIX, :], preferred_element_type=F32)
                + jnp.dot(xo_ref[0], wo_ref[MIX:, :], preferred_element_type=F32))


def _gqa_layer(x, layer, nw, w_in, qn_t, kn_t, sink, cos, sin_signed, xqn_t, mk, mv, w_out):
    q, k2, v2, xo = _gqa_proj(x, nw, w_in, qn_t, kn_t, cos, sin_signed, xqn_t, mk, mv)
    return _win_attn(x, q, k2, v2, xo, sink, layer, w_out)


def _gqa_proj(x, nw, w_in, qn_t, kn_t, cos, sin_signed, xqn_t, mk, mv):
    B, S, _ = x.shape
    tm = TM_PROJ
    tok = lambda w: pl.BlockSpec((1, tm, w), lambda b, j: (b, j, 0))
    full = lambda a: pl.BlockSpec(a.shape, lambda b, j: (0,) * a.ndim)
    memspec = pl.BlockSpec((1, 1, X_HEADS, N_MEM, X_W), lambda b, j: (1, b, 0, 0, 0))
    act = lambda w, dt: jax.ShapeDtypeStruct((B, S, w), dt)
    return pl.pallas_call(
        _gq_proj_kernel,
        grid=(B, S // tm),
        in_specs=[tok(D_MODEL), full(nw), full(w_in), full(qn_t), full(kn_t),
                  pl.BlockSpec((tm, LANES), lambda b, j: (j, 0)),
                  pl.BlockSpec((tm, LANES), lambda b, j: (j, 0)),
                  full(xqn_t), memspec, memspec],
        out_specs=[tok(MIX), tok(2 * KV_W), tok(2 * KV_W), tok(X_W)],
        out_shape=[act(MIX, BF16), act(2 * KV_W, BF16), act(2 * KV_W, BF16), act(X_W, BF16)],
        compiler_params=_params(2),
        name="gqa_proj",
    )(x, nw, w_in, qn_t, kn_t, cos, sin_signed, xqn_t, mk, mv)


def _win_attn(x, q, k2, v2, xo, sink, layer, w_out):
    B, S, _ = x.shape
    ta = TM_ATT
    tok = lambda w, t: pl.BlockSpec((1, t, w), lambda b, j: (b, j, 0))
    full = lambda a: _layer_spec(a, layer)
    act = lambda w, dt: jax.ShapeDtypeStruct((B, S, w), dt)
    r = ta // WINDOW
    last = S // WINDOW - 1
    prev = pl.BlockSpec((1, WINDOW, 2 * KV_W), lambda b, j: (b, jnp.maximum(j * r - 1, 0), 0))
    nxt = pl.BlockSpec((1, WINDOW, 2 * KV_W), lambda b, j: (b, jnp.minimum((j + 1) * r, last), 0))
    return pl.pallas_call(
        functools.partial(_win_attn_kernel, seq_len=S),
        grid=(B, S // ta),
        in_specs=[pl.BlockSpec(memory_space=pltpu.SMEM),
                  tok(D_MODEL, ta), tok(MIX, ta),
                  tok(2 * KV_W, ta), prev, nxt, tok(2 * KV_W, ta), prev, nxt,
                  tok(X_W, ta), full(w_out)],
        out_specs=tok(D_MODEL, ta),
        out_shape=act(D_MODEL, F32),
        scratch_shapes=[pltpu.VMEM((2 * N_KV, ta + 2 * WINDOW, LANES), BF16),
                        pltpu.VMEM((2 * N_KV, ta + 2 * WINDOW, LANES), BF16),
                        pltpu.VMEM((ta // WINDOW, WINDOW, 3 * WINDOW), F32),
                        pltpu.VMEM((ta, MIX), BF16)],
        compiler_params=_params(2),
        name="win_attn_out",
    )(sink, x, q, k2, k2, k2, v2, v2, v2, xo, w_out)


def _rope_tables(seq_len):
    freqs = ROPE_THETA ** (-jnp.arange(0, HD, 2, dtype=F32) / HD)
    ang = jnp.arange(seq_len, dtype=F32)[:, None] * freqs[None, :]
    cos, sin = jnp.cos(ang), jnp.sin(ang)
    return (jnp.tile(cos, (1, 2 * LANES // HD)),
            jnp.tile(jnp.concatenate([-sin, sin], axis=1), (1, LANES // HD)))


def _trunk(x, mem, p, cos, sin_signed):
    mk, mv = _memkv(mem, p["norm_mem"], p["x_w_kv"], p["x_kn"])
    for i in range(DEPTH):
        if i % 2 == 0:
            a = i // 2
            assert a == 0 and i == 0, "the HGRN2 kernels read layer 0's forget-gate lower bounds"
            x = _hgrn2_layer(x, i, p["norm_mix"][i], p["hg_w_in"][a], p["hg_lb"], p["hg_gn"][a],
                             p["x_qn"][i], mk, mv, p["w_out"])
        else:
            b = i // 2
            assert i == 1, "the attention kernels read layer 1's memory keys and values"
            x = _gqa_layer(x, i, p["norm_mix"][i], p["gq_w_in"][b], p["gq_qn"][b], p["gq_kn"][b],
                           p["gq_sink"][b], cos, sin_signed, p["x_qn"][i], mk, mv, p["w_out"])
        x = _ffn(x, i, p["norm_ffn"], p["ffn_w_up"], p["ffn_conv_w"], p["ffn_conv_b"],
                 p["ffn_w_down"])
    return x


def _prepare(norm_mix, norm_mem, norm_ffn, hg_w_in, hg_lb, hg_gn, gq_w_in, gq_qn, gq_kn, gq_sink,
             x_w_kv, x_qn, x_kn, w_out, ffn_w_up, ffn_conv_w, ffn_conv_b, ffn_w_down):
    row = lambda a: a.reshape(a.shape[0], 1, a.shape[-1])
    return {
        "norm_mix": row(norm_mix), "norm_mem": row(norm_mem), "norm_ffn": row(norm_ffn),
        "hg_w_in": hg_w_in.astype(BF16), "hg_lb": hg_lb, "hg_gn": row(hg_gn),
        "gq_w_in": gq_w_in.astype(BF16),
        "gq_qn": row(jnp.tile(gq_qn, (1, N_Q))), "gq_kn": row(jnp.tile(gq_kn, (1, N_KV))),
        "gq_sink": gq_sink,
        "x_w_kv": x_w_kv.astype(BF16),
        "x_qn": row(jnp.tile(x_qn, (1, X_HEADS))), "x_kn": row(jnp.tile(x_kn, (1, X_HEADS))),
        "w_out": w_out.astype(BF16),
        "ffn_w_up": ffn_w_up.astype(BF16), "ffn_conv_w": ffn_conv_w, "ffn_conv_b": row(ffn_conv_b),
        "ffn_w_down": ffn_w_down.astype(BF16),
    }


def kernel(x_prompt, x_sample, mem_prompt, mem_sample, norm_mix, norm_mem, norm_ffn, hg_w_in, hg_lb, hg_gn, gq_w_in, gq_qn, gq_kn, gq_sink, x_w_kv, x_qn, x_kn, w_out, ffn_w_up, ffn_conv_w, ffn_conv_b, ffn_w_down):
    p = _prepare(norm_mix, norm_mem, norm_ffn, hg_w_in, hg_lb, hg_gn, gq_w_in, gq_qn, gq_kn,
                 gq_sink, x_w_kv, x_qn, x_kn, w_out, ffn_w_up, ffn_conv_w, ffn_conv_b, ffn_w_down)
    cos, sin_signed = _rope_tables(max(x_prompt.shape[1], x_sample.shape[1]))
    return (_trunk(x_prompt, mem_prompt, p, cos, sin_signed),
            _trunk(x_sample, mem_sample, p, cos, sin_signed))
```

```python
import functools

import jax
import jax.numpy as jnp
from jax import lax
from jax.experimental import pallas as pl
from jax.experimental.pallas import tpu as pltpu

F32 = jnp.float32
BF16 = jnp.bfloat16

D_MODEL = 1024
DEPTH = 2
MIX = 768
N_MEM = 256
X_HEADS = 4
X_HD = 64
X_W = X_HEADS * X_HD
HG_HEADS = 6
HG_DK = 128
HG_CHUNK = 64
N_Q = 12
N_KV = 4
HD = 64
GROUP = N_Q // N_KV
KV_W = N_KV * HD
WINDOW = 128
ROPE_THETA = 10000.0
D_FF = 2816
EPS = 1e-6

LANES = 128
MXU_N = 256
VMEM_LIMIT = 56 * 1024 * 1024

TM_HG = 1024
TM_HG_FWD = 512
HG_SUB = 256
TM_PROJ = 1024
PROJ_SUB = 256
TM_ATT = 512
ATT_GROUPS = 6
TM_FFN = 1024
FFN_HALO = 8
FFN_TN = 256


def _params(n_axes):
    return pltpu.CompilerParams(
        dimension_semantics=("arbitrary",) * n_axes, vmem_limit_bytes=VMEM_LIMIT)


def _rms_rows(x, w):
    ms = jnp.mean(x * x, axis=-1, keepdims=True)
    return x * lax.rsqrt(ms + EPS) * w


def _head_ones(width, hd):
    r = lax.broadcasted_iota(jnp.int32, (width, width), 0) // hd
    c = lax.broadcasted_iota(jnp.int32, (width, width), 1) // hd
    return jnp.where(r == c, 1.0, 0.0).astype(BF16)


def _head_rms(x, hd):
    ones = _head_ones(MXU_N, hd)
    parts = []
    for g in range(x.shape[-1] // MXU_N):
        xs = x[:, g * MXU_N:(g + 1) * MXU_N]
        ss = jnp.dot((xs * xs).astype(BF16), ones, preferred_element_type=F32)
        parts.append(xs * lax.rsqrt(ss * (1.0 / hd) + EPS))
    return parts[0] if len(parts) == 1 else jnp.concatenate(parts, axis=-1)


def _dot_nt(a, b):
    return lax.dot_general(a, b, (((1,), (1,)), ((), ())), preferred_element_type=F32)


def _dot_tn(a, b):
    return lax.dot_general(a, b, (((0,), (0,)), ((), ())), preferred_element_type=F32)


def _round_robin(items, stages):
    inflight = [None] * len(stages)
    for tick in range(len(items) + len(stages) - 1):
        inflight = [(items[tick], None) if tick < len(items) else None] + inflight[:-1]
        for k, fn in enumerate(stages):
            if inflight[k] is not None:
                item, state = inflight[k]
                inflight[k] = (item, fn(item, state))


def _memkv_kernel(mem_ref, nw_ref, wkv_ref, kn_ref, mk_ref, mv_ref):
    mh = _rms_rows(mem_ref[0], nw_ref[0]).astype(BF16)
    kv = jnp.dot(mh, wkv_ref[0], preferred_element_type=F32)
    k = _head_rms(kv[:, :X_W], X_HD) * kn_ref[0]
    v = kv[:, X_W:]
    head = lax.broadcasted_iota(jnp.int32, (N_MEM, X_W), 1) // X_HD
    for h in range(X_HEADS):
        mk_ref[0, 0, h] = jnp.where(head == h, k, 0.0).astype(BF16)
        mv_ref[0, 0, h] = jnp.where(head == h, v, 0.0).astype(BF16)


def _memkv(mem, norm_mem, w_kv, kn_t):
    B = mem.shape[0]
    out = jax.ShapeDtypeStruct((DEPTH, B, X_HEADS, N_MEM, X_W), BF16)
    return pl.pallas_call(
        _memkv_kernel,
        grid=(DEPTH, B),
        in_specs=[
            pl.BlockSpec((1, N_MEM, D_MODEL), lambda l, b: (b, 0, 0)),
            pl.BlockSpec((1, 1, D_MODEL), lambda l, b: (l, 0, 0)),
            pl.BlockSpec((1, D_MODEL, 2 * X_W), lambda l, b: (l, 0, 0)),
            pl.BlockSpec((1, 1, X_W), lambda l, b: (l, 0, 0)),
        ],
        out_specs=[
            pl.BlockSpec((1, 1, X_HEADS, N_MEM, X_W), lambda l, b: (l, b, 0, 0, 0)),
            pl.BlockSpec((1, 1, X_HEADS, N_MEM, X_W), lambda l, b: (l, b, 0, 0, 0)),
        ],
        out_shape=[out, out],
        compiler_params=_params(2),
        name="mem_kv",
    )(mem, norm_mem, w_kv, kn_t)


def _mem_scores(xq, qn, mk_ref):
    q = ((_head_rms(xq, X_HD) * qn) * (X_HD ** -0.5)).astype(BF16)
    return [_dot_nt(q, mk_ref[h]) for h in range(X_HEADS)]


def _mem_out(scores, mv_ref):
    R = scores[0].shape[0]
    s = jnp.concatenate(scores, axis=0)
    p = jnp.exp(s - jnp.max(s, axis=-1, keepdims=True))
    pn = (p / jnp.sum(p, axis=-1, keepdims=True)).astype(BF16)
    acc = None
    for h in range(len(scores)):
        o = jnp.dot(pn[h * R:(h + 1) * R], mv_ref[h], preferred_element_type=F32)
        acc = o if acc is None else acc + o
    return acc


def _lower_bound(lb_ref, direction, layer):
    r = lb_ref[direction]
    e = jnp.exp(r - jnp.max(r, axis=0, keepdims=True))
    return jnp.sum(e[:layer + 1], axis=0, keepdims=True) / jnp.sum(e, axis=0, keepdims=True)


def _chunk_tri(n, reverse):
    r = lax.broadcasted_iota(jnp.int32, (n, n), 0)
    c = lax.broadcasted_iota(jnp.int32, (n, n), 1)
    same = (r // HG_CHUNK) == (c // HG_CHUNK)
    order = (c >= r) if reverse else (c <= r)
    return jnp.where(same & order, 1.0, 0.0).astype(BF16)


def _gates(f_raw, lb):
    f = lb + (1.0 - lb) * jax.nn.sigmoid(f_raw)
    return jnp.log(f), 1.0 - f


def _gla_prep(qs, g, k, qt_ref, kt_ref, reverse):
    R = qs.shape[0]
    tri = _chunk_tri(R, reverse)
    g_hi = g.astype(BF16)
    g_lo = (g - g_hi.astype(F32)).astype(BF16)
    b = (jnp.dot(tri, g_hi, preferred_element_type=F32)
         + jnp.dot(tri, g_lo, preferred_element_type=F32))
    qt_ref[...] = (qs * jnp.exp(b)).astype(BF16)
    kt_ref[...] = (k * jnp.exp(-b)).astype(BF16)
    decays = []
    for c in range(R // HG_CHUNK):
        end_row = c * HG_CHUNK if reverse else (c + 1) * HG_CHUNK - 1
        decays.append(jnp.exp(b[end_row:end_row + 1, :]))
    return decays


def _gla_chunks(vb, decays, st_ref, qt_ref, kt_ref, o_ref, reverse):
    n_chunks = len(decays)
    ri = lax.broadcasted_iota(jnp.int32, (HG_CHUNK, HG_CHUNK), 0)
    ci = lax.broadcasted_iota(jnp.int32, (HG_CHUNK, HG_CHUNK), 1)
    keep = (ci >= ri) if reverse else (ci <= ri)
    order = list(range(n_chunks - 1, -1, -1) if reverse else range(n_chunks))

    def local(c):
        r0 = c * HG_CHUNK
        out = []
        for h in range(HG_HEADS):
            c0 = h * HG_DK
            qt = qt_ref[r0:r0 + HG_CHUNK, c0:c0 + HG_DK]
            kt = kt_ref[r0:r0 + HG_CHUNK, c0:c0 + HG_DK]
            vv = vb[r0:r0 + HG_CHUNK, c0:c0 + HG_DK]
            sc = jnp.where(keep, _dot_nt(qt, kt), 0.0).astype(BF16)
            out.append((qt, vv, sc, _dot_tn(vv, kt)))
        return out

    nxt = local(order[0])
    for n, c in enumerate(order):
        cur = nxt
        if n + 1 < n_chunks:
            nxt = local(order[n + 1])
        r0 = c * HG_CHUNK
        for h, (qt, vv, sc, kv) in enumerate(cur):
            c0 = h * HG_DK
            st = st_ref[c0:c0 + HG_DK, :]
            o_ref[r0:r0 + HG_CHUNK, c0:c0 + HG_DK] = (
                jnp.dot(sc, vv, preferred_element_type=F32) + _dot_nt(qt, st.astype(BF16)))
            st_ref[c0:c0 + HG_DK, :] = (st + kv) * decays[c][:, c0:c0 + HG_DK]


def _hg_fwd_kernel(x_ref, nw_ref, w_ref, lb_ref, qn_ref, mk_ref, mv_ref,
                   qs_ref, v_ref, gb_ref, kb_ref, sg_ref, of_ref, xo_ref,
                   st_ref, qt_ref, kt_ref):
    @pl.when(pl.program_id(1) == 0)
    def _():
        st_ref[...] = jnp.zeros_like(st_ref)

    lb = _lower_bound(lb_ref, 0, 0)
    lb_back = _lower_bound(lb_ref, 1, 0)
    rows = lambda i: slice(i * HG_SUB, (i + 1) * HG_SUB)

    def project(i, _):
        r = rows(i)
        h = _rms_rows(x_ref[0, r, :], nw_ref[...]).astype(BF16)
        proj = lambda c, w=MIX: jnp.dot(h, w_ref[:, c * MIX:c * MIX + w],
                                        preferred_element_type=F32)
        f_raw = proj(1)
        q = proj(0)
        cx = proj(5, X_W)
        vb = proj(3).astype(BF16)
        go = proj(4)
        gb_ref[0, r, :], kb_ref[0, r, :] = _gates(proj(2), lb_back)
        qs = q * jax.nn.sigmoid(q)
        qs_ref[0, r, :] = qs.astype(BF16)
        v_ref[0, r, :] = vb
        sg_ref[0, r, :] = (go * jax.nn.sigmoid(go)).astype(BF16)
        return qs, f_raw, cx, vb

    def gates(i, state):
        qs, f_raw, cx, vb = state
        r = rows(i)
        mem_s = _mem_scores(cx, qn_ref[...], mk_ref.at[0, 0])
        decays = _gla_prep(qs, *_gates(f_raw, lb), qt_ref.at[r], kt_ref.at[r], reverse=False)
        return mem_s, decays, vb

    def recur(i, state):
        mem_s, decays, vb = state
        r = rows(i)
        xo_ref[0, r, :] = _mem_out(mem_s, mv_ref.at[0, 0]).astype(BF16)
        _gla_chunks(vb, decays, st_ref, qt_ref.at[r], kt_ref.at[r], of_ref.at[0, r],
                    reverse=False)

    _round_robin(list(range(x_ref.shape[1] // HG_SUB)), [project, gates, recur])


def _hg_bwd_kernel(x_ref, qs_ref, v_ref, gb_ref, kb_ref, sg_ref, of_ref, xo_ref, gn_ref, wo_ref,
                   y_ref, st_ref, qt_ref, kt_ref, ob_ref):
    @pl.when(pl.program_id(1) == 0)
    def _():
        st_ref[...] = jnp.zeros_like(st_ref)

    rows = lambda i: slice(i * HG_SUB, (i + 1) * HG_SUB)

    def gates(i, _):
        r = rows(i)
        return _gla_prep(qs_ref[0, r, :].astype(F32), gb_ref[0, r, :], kb_ref[0, r, :],
                         qt_ref.at[r], kt_ref.at[r], reverse=True)

    def recur(i, decays):
        r = rows(i)
        _gla_chunks(v_ref[0, r, :], decays, st_ref, qt_ref.at[r], kt_ref.at[r], ob_ref.at[r],
                    reverse=True)

    def output(i, _):
        r = rows(i)
        o = of_ref[0, r, :] + ob_ref[r, :]
        on = (_rms_rows(o, gn_ref[...]) * sg_ref[0, r, :].astype(F32)).astype(BF16)
        y_ref[0, r, :] = (x_ref[0, r, :]
                          + jnp.dot(on, wo_ref[:MIX, :], preferred_element_type=F32)
                          + jnp.dot(xo_ref[0, r, :], wo_ref[MIX:, :], preferred_element_type=F32))

    _round_robin(list(range(x_ref.shape[1] // HG_SUB - 1, -1, -1)), [gates, recur, output])


def _layer_spec(a, layer):
    return pl.BlockSpec((None,) + a.shape[1:], lambda b, j: (layer,) + (0,) * (a.ndim - 1),
                        pipeline_mode=pl.Buffered(1))


def _hgrn2_layer(x, layer, nw, w_in, hg_lb, gn, qn_t, mk, mv, w_out):
    B, S, _ = x.shape
    tm = TM_HG_FWD
    nb = S // tm
    tok = lambda w: pl.BlockSpec((1, tm, w), lambda b, j: (b, j, 0))
    full = lambda a: pl.BlockSpec(a.shape, lambda b, j: (0,) * a.ndim)
    memspec = pl.BlockSpec((1, 1, X_HEADS, N_MEM, X_W), lambda b, j: (0, b, 0, 0, 0))
    act = lambda w, dt: jax.ShapeDtypeStruct((B, S, w), dt)
    qs, v, gb, kb, sg, of, xo = pl.pallas_call(
        _hg_fwd_kernel,
        grid=(B, nb),
        in_specs=[tok(D_MODEL), full(nw), full(w_in), full(hg_lb), full(qn_t), memspec, memspec],
        out_specs=[tok(MIX), tok(MIX), tok(MIX), tok(MIX), tok(MIX), tok(MIX), tok(X_W)],
        out_shape=[act(MIX, BF16), act(MIX, BF16), act(MIX, F32), act(MIX, F32), act(MIX, BF16),
                   act(MIX, F32), act(X_W, BF16)],
        scratch_shapes=[pltpu.VMEM((MIX, HG_DK), F32), pltpu.VMEM((tm, MIX), BF16),
                        pltpu.VMEM((tm, MIX), BF16)],
        compiler_params=_params(2),
        name="hgrn2_fwd",
    )(x, nw, w_in, hg_lb, qn_t, mk, mv)

    tm = TM_HG
    nb = S // tm
    rtok = lambda w: pl.BlockSpec((1, tm, w), lambda b, j: (b, nb - 1 - j, 0))
    return pl.pallas_call(
        _hg_bwd_kernel,
        grid=(B, nb),
        in_specs=[rtok(D_MODEL), rtok(MIX), rtok(MIX), rtok(MIX), rtok(MIX), rtok(MIX), rtok(MIX),
                  rtok(X_W), full(gn), _layer_spec(w_out, layer)],
        out_specs=rtok(D_MODEL),
        out_shape=act(D_MODEL, F32),
        scratch_shapes=[pltpu.VMEM((MIX, HG_DK), F32), pltpu.VMEM((tm, MIX), BF16),
                        pltpu.VMEM((tm, MIX), BF16), pltpu.VMEM((tm, MIX), F32)],
        compiler_params=_params(2),
        name="hgrn2_bwd_out",
    )(x, qs, v, gb, kb, sg, of, xo, gn, w_out)


def _ffn_kernel(x_ref, xp_ref, xn_ref, nw_ref, wu_ref, cw_ref, cb_ref, wd_ref, y_ref,
                h_ref, u_ref, a_ref):
    j = pl.program_id(1)
    nb = pl.num_programs(1)
    tm = x_ref.shape[1]
    rows = tm + 2 * FFN_HALO
    nw = nw_ref[...]
    x = x_ref[0]
    halo = jnp.concatenate([xn_ref[0] * jnp.where(j < nb - 1, 1.0, 0.0),
                            xp_ref[0] * jnp.where(j > 0, 1.0, 0.0)], axis=0)
    h_ref[0:tm, :] = _rms_rows(x, nw).astype(BF16)
    h_ref[tm:, :] = _rms_rows(halo, nw).astype(BF16)
    hh = h_ref[...]

    def up(t):
        for half in range(2):
            c0 = half * D_FF + t * FFN_TN
            u_ref[t % 2, half] = jnp.dot(hh, wu_ref[:, c0:c0 + FFN_TN], preferred_element_type=F32)

    def conv(t, half):
        c0 = half * D_FF + t * FFN_TN
        u = u_ref[t % 2, half]
        w = cw_ref[:, c0:c0 + FFN_TN]
        return (pltpu.roll(u, 1, 0)[0:tm] * w[0:1] + u[0:tm] * w[1:2]
                + pltpu.roll(u, rows - 1, 0)[0:tm] * w[2:3] + cb_ref[:, c0:c0 + FFN_TN])

    n_tiles = D_FF // FFN_TN
    up(0)
    for t in range(n_tiles):
        if t + 1 < n_tiles:
            up(t + 1)
        gate = conv(t, 0)
        val = conv(t, 1)
        a_ref[:, t * FFN_TN:(t + 1) * FFN_TN] = (gate * jax.nn.sigmoid(gate) * val).astype(BF16)
    y_ref[0] = x + jnp.dot(a_ref[...], wd_ref[...], preferred_element_type=F32)


def _ffn(x, layer, nw, w_up, conv_w, conv_b, w_down):
    B, S, _ = x.shape
    tm = TM_FFN
    nb = S // tm
    r = tm // FFN_HALO
    last = S // FFN_HALO - 1
    full = lambda a: _layer_spec(a, layer)
    return pl.pallas_call(
        _ffn_kernel,
        grid=(B, nb),
        in_specs=[
            pl.BlockSpec((1, tm, D_MODEL), lambda b, j: (b, j, 0)),
            pl.BlockSpec((1, FFN_HALO, D_MODEL), lambda b, j: (b, jnp.maximum(j * r - 1, 0), 0)),
            pl.BlockSpec((1, FFN_HALO, D_MODEL),
                         lambda b, j: (b, jnp.minimum((j + 1) * r, last), 0)),
            full(nw), full(w_up), full(conv_w), full(conv_b), full(w_down),
        ],
        out_specs=pl.BlockSpec((1, tm, D_MODEL), lambda b, j: (b, j, 0)),
        out_shape=jax.ShapeDtypeStruct((B, S, D_MODEL), F32),
        scratch_shapes=[pltpu.VMEM((tm + 2 * FFN_HALO, D_MODEL), BF16),
                        pltpu.VMEM((2, 2, tm + 2 * FFN_HALO, FFN_TN), F32),
                        pltpu.VMEM((tm, D_FF), BF16)],
        compiler_params=_params(2),
        name="conv_glu",
    )(x, x, x, nw, w_up, conv_w, conv_b, w_down)


def _rope(x, cos, sin_signed, first_half):
    rot = jnp.where(first_half, pltpu.roll(x, LANES - HD // 2, 1), pltpu.roll(x, HD // 2, 1))
    return x * cos + rot * sin_signed


def _gq_proj_kernel(x_ref, nw_ref, w_ref, qn_ref, kn_ref, cos_ref, sin_ref, xqn_ref,
                    mk_ref, mv_ref, q_ref, k_ref, v_ref, xo_ref):
    rows = lambda i: slice(i * PROJ_SUB, (i + 1) * PROJ_SUB)

    def project(i, _):
        h = _rms_rows(x_ref[0, rows(i), :], nw_ref[...]).astype(BF16)
        return jnp.dot(h, w_ref[...], preferred_element_type=F32)

    def rotate(i, cols):
        r = rows(i)
        cos = cos_ref[r, :]
        sin = sin_ref[r, :]
        first_half = (lax.broadcasted_iota(jnp.int32, cos.shape, 1) % HD) < HD // 2
        qn = _head_rms(cols[:, :MIX], HD) * qn_ref[...]
        for g in range(MIX // LANES):
            sl = slice(g * LANES, (g + 1) * LANES)
            q_ref[0, r, sl] = (_rope(qn[:, sl], cos, sin, first_half) * (HD ** -0.5)).astype(BF16)
        kn = _head_rms(cols[:, MIX:MIX + KV_W], HD) * kn_ref[...]
        v = cols[:, MIX + KV_W:MIX + 2 * KV_W]
        for g in range(KV_W // LANES):
            sl = slice(g * LANES, (g + 1) * LANES)
            sw = slice(KV_W + g * LANES, KV_W + (g + 1) * LANES)
            kr = _rope(kn[:, sl], cos, sin, first_half)
            k_ref[0, r, sl] = kr.astype(BF16)
            k_ref[0, r, sw] = pltpu.roll(kr, HD, 1).astype(BF16)
            v_ref[0, r, sl] = v[:, sl].astype(BF16)
            v_ref[0, r, sw] = pltpu.roll(v[:, sl], HD, 1).astype(BF16)
        return _mem_scores(cols[:, MIX + 2 * KV_W:], xqn_ref[...], mk_ref.at[0, 0])

    def memory(i, mem_s):
        xo_ref[0, rows(i), :] = _mem_out(mem_s, mv_ref.at[0, 0]).astype(BF16)

    _round_robin(list(range(x_ref.shape[1] // PROJ_SUB)), [project, rotate, memory])


def _win_attn_kernel(sink_ref, x_ref, q_ref, k_ref, kp_ref, kn_ref, v_ref, vp_ref, vn_ref,
                     xo_ref, wo_ref, y_ref, kv_ref, vv_ref, bias_ref, o_ref, *, seq_len):
    j = pl.program_id(1)
    tm = x_ref.shape[1]
    rows = tm + 2 * WINDOW
    half = lax.broadcasted_iota(jnp.int32, (rows, LANES), 1) // HD
    for src, prev, nxt, dst in ((k_ref, kp_ref, kn_ref, kv_ref), (v_ref, vp_ref, vn_ref, vv_ref)):
        for kvh in range(N_KV):
            for hq in range(2):
                c0 = (kvh // 2) * LANES + (0 if kvh % 2 == hq else KV_W)
                cat = jnp.concatenate([prev[0, :, c0:c0 + LANES], src[0, :, c0:c0 + LANES],
                                       nxt[0, :, c0:c0 + LANES]], axis=0)
                dst[kvh * 2 + hq] = jnp.where(half == hq, cat, jnp.zeros_like(cat))

    qi = lax.broadcasted_iota(jnp.int32, (WINDOW, 3 * WINDOW), 0)
    kj = lax.broadcasted_iota(jnp.int32, (WINDOW, 3 * WINDOW), 1)
    n_sub = tm // WINDOW
    for i in range(n_sub):
        kpos = j * tm + (i - 1) * WINDOW + kj
        ok = (kj - qi >= 0) & (kj - qi <= 2 * WINDOW) & (kpos >= 0) & (kpos < seq_len)
        bias_ref[i] = jnp.where(ok, 0.0, -jnp.inf)

    slab = lambda g, hq: ((2 * g + hq) // GROUP) * 2 + hq

    def scores(i, gs):
        r0 = i * WINDOW
        out = []
        for g in gs:
            qp = q_ref[0, r0:r0 + WINDOW, g * LANES:(g + 1) * LANES]
            out += [_dot_nt(qp, kv_ref[slab(g, hq), r0:r0 + 3 * WINDOW, :]) for hq in range(2)]
        return out

    n_heads = 2 * ATT_GROUPS
    row_head = lax.broadcasted_iota(jnp.int32, (n_heads * WINDOW, 1), 0) // WINDOW
    groups = [tuple(range(g0, g0 + ATT_GROUPS)) for g0 in range(0, N_Q // 2, ATT_GROUPS)]
    work = [(i, gs) for i in range(n_sub) for gs in groups]

    def stage_max(item, s_raw):
        i, gs = item
        s = jnp.concatenate(s_raw, axis=0) + jnp.concatenate([bias_ref[i]] * n_heads, axis=0)
        sk = jnp.zeros((n_heads * WINDOW, 1), F32)
        for n, g in enumerate(gs):
            for hq in range(2):
                sk = jnp.where(row_head == 2 * n + hq, sink_ref[2 * g + hq], sk)
        return s, sk, jnp.maximum(jnp.max(s, axis=-1, keepdims=True), sk)

    def stage_exp(item, st):
        s, sk, m = st
        p = jnp.exp(s - m)
        return p, jnp.sum(p, axis=-1, keepdims=True) + jnp.exp(sk - m)

    def stage_out(item, st):
        i, gs = item
        p, den = st
        r0 = i * WINDOW
        pn = (p / den).astype(BF16)
        for n, g in enumerate(gs):
            acc = None
            for hq in range(2):
                rs = slice((2 * n + hq) * WINDOW, (2 * n + hq + 1) * WINDOW)
                o = jnp.dot(pn[rs], vv_ref[slab(g, hq), r0:r0 + 3 * WINDOW, :],
                            preferred_element_type=F32)
                acc = o if acc is None else acc + o
            o_ref[r0:r0 + WINDOW, g * LANES:(g + 1) * LANES] = acc.astype(BF16)

    _round_robin(work, [lambda item, _: scores(*item), stage_max, stage_exp, stage_out])
    y_ref[0] = (x_ref[0]
                + jnp.dot(o_ref[...], wo_ref[:MIX, :], preferred_element_type=F32)
                + jnp.dot(xo_ref[0], wo_ref[MIX:, :], preferred_element_type=F32))


def _gqa_layer(x, layer, nw, w_in, qn_t, kn_t, sink, cos, sin_signed, xqn_t, mk, mv, w_out):
    q, k2, v2, xo = _gqa_proj(x, nw, w_in, qn_t, kn_t, cos, sin_signed, xqn_t, mk, mv)
    return _win_attn(x, q, k2, v2, xo, sink, layer, w_out)


def _gqa_proj(x, nw, w_in, qn_t, kn_t, cos, sin_signed, xqn_t, mk, mv):
    B, S, _ = x.shape
    tm = TM_PROJ
    tok = lambda w: pl.BlockSpec((1, tm, w), lambda b, j: (b, j, 0))
    full = lambda a: pl.BlockSpec(a.shape, lambda b, j: (0,) * a.ndim)
    memspec = pl.BlockSpec((1, 1, X_HEADS, N_MEM, X_W), lambda b, j: (1, b, 0, 0, 0))
    act = lambda w, dt: jax.ShapeDtypeStruct((B, S, w), dt)
    return pl.pallas_call(
        _gq_proj_kernel,
        grid=(B, S // tm),
        in_specs=[tok(D_MODEL), full(nw), full(w_in), full(qn_t), full(kn_t),
                  pl.BlockSpec((tm, LANES), lambda b, j: (j, 0)),
                  pl.BlockSpec((tm, LANES), lambda b, j: (j, 0)),
                  full(xqn_t), memspec, memspec],
        out_specs=[tok(MIX), tok(2 * KV_W), tok(2 * KV_W), tok(X_W)],
        out_shape=[act(MIX, BF16), act(2 * KV_W, BF16), act(2 * KV_W, BF16), act(X_W, BF16)],
        compiler_params=_params(2),
        name="gqa_proj",
    )(x, nw, w_in, qn_t, kn_t, cos, sin_signed, xqn_t, mk, mv)


def _win_attn(x, q, k2, v2, xo, sink, layer, w_out):
    B, S, _ = x.shape
    ta = TM_ATT
    tok = lambda w, t: pl.BlockSpec((1, t, w), lambda b, j: (b, j, 0))
    full = lambda a: _layer_spec(a, layer)
    act = lambda w, dt: jax.ShapeDtypeStruct((B, S, w), dt)
    r = ta // WINDOW
    last = S // WINDOW - 1
    prev = pl.BlockSpec((1, WINDOW, 2 * KV_W), lambda b, j: (b, jnp.maximum(j * r - 1, 0), 0))
    nxt = pl.BlockSpec((1, WINDOW, 2 * KV_W), lambda b, j: (b, jnp.minimum((j + 1) * r, last), 0))
    return pl.pallas_call(
        functools.partial(_win_attn_kernel, seq_len=S),
        grid=(B, S // ta),
        in_specs=[pl.BlockSpec(memory_space=pltpu.SMEM),
                  tok(D_MODEL, ta), tok(MIX, ta),
                  tok(2 * KV_W, ta), prev, nxt, tok(2 * KV_W, ta), prev, nxt,
                  tok(X_W, ta), full(w_out)],
        out_specs=tok(D_MODEL, ta),
        out_shape=act(D_MODEL, F32),
        scratch_shapes=[pltpu.VMEM((2 * N_KV, ta + 2 * WINDOW, LANES), BF16),
                        pltpu.VMEM((2 * N_KV, ta + 2 * WINDOW, LANES), BF16),
                        pltpu.VMEM((ta // WINDOW, WINDOW, 3 * WINDOW), F32),
                        pltpu.VMEM((ta, MIX), BF16)],
        compiler_params=_params(2),
        name="win_attn_out",
    )(sink, x, q, k2, k2, k2, v2, v2, v2, xo, w_out)


def _rope_tables(seq_len):
    freqs = ROPE_THETA ** (-jnp.arange(0, HD, 2, dtype=F32) / HD)
    ang = jnp.arange(seq_len, dtype=F32)[:, None] * freqs[None, :]
    cos, sin = jnp.cos(ang), jnp.sin(ang)
    return (jnp.tile(cos, (1, 2 * LANES // HD)),
            jnp.tile(jnp.concatenate([-sin, sin], axis=1), (1, LANES // HD)))


def _trunk(x, mem, p, cos, sin_signed):
    mk, mv = _memkv(mem, p["norm_mem"], p["x_w_kv"], p["x_kn"])
    for i in range(DEPTH):
        if i % 2 == 0:
            a = i // 2
            assert a == 0 and i == 0, "the HGRN2 kernels read layer 0's forget-gate lower bounds"
            x = _hgrn2_layer(x, i, p["norm_mix"][i], p["hg_w_in"][a], p["hg_lb"], p["hg_gn"][a],
                             p["x_qn"][i], mk, mv, p["w_out"])
        else:
            b = i // 2
            assert i == 1, "the attention kernels read layer 1's memory keys and values"
            x = _gqa_layer(x, i, p["norm_mix"][i], p["gq_w_in"][b], p["gq_qn"][b], p["gq_kn"][b],
                           p["gq_sink"][b], cos, sin_signed, p["x_qn"][i], mk, mv, p["w_out"])
        x = _ffn(x, i, p["norm_ffn"], p["ffn_w_up"], p["ffn_conv_w"], p["ffn_conv_b"],
                 p["ffn_w_down"])
    return x


def _prepare(norm_mix, norm_mem, norm_ffn, hg_w_in, hg_lb, hg_gn, gq_w_in, gq_qn, gq_kn, gq_sink,
             x_w_kv, x_qn, x_kn, w_out, ffn_w_up, ffn_conv_w, ffn_conv_b, ffn_w_down):
    row = lambda a: a.reshape(a.shape[0], 1, a.shape[-1])
    return {
        "norm_mix": row(norm_mix), "norm_mem": row(norm_mem), "norm_ffn": row(norm_ffn),
        "hg_w_in": hg_w_in.astype(BF16), "hg_lb": hg_lb, "hg_gn": row(hg_gn),
        "gq_w_in": gq_w_in.astype(BF16),
        "gq_qn": row(jnp.tile(gq_qn, (1, N_Q))), "gq_kn": row(jnp.tile(gq_kn, (1, N_KV))),
        "gq_sink": gq_sink,
        "x_w_kv": x_w_kv.astype(BF16),
        "x_qn": row(jnp.tile(x_qn, (1, X_HEADS))), "x_kn": row(jnp.tile(x_kn, (1, X_HEADS))),
        "w_out": w_out.astype(BF16),
        "ffn_w_up": ffn_w_up.astype(BF16), "ffn_conv_w": ffn_conv_w, "ffn_conv_b": row(ffn_conv_b),
        "ffn_w_down": ffn_w_down.astype(BF16),
    }


def kernel(x_prompt, x_sample, mem_prompt, mem_sample, norm_mix, norm_mem, norm_ffn, hg_w_in, hg_lb, hg_gn, gq_w_in, gq_qn, gq_kn, gq_sink, x_w_kv, x_qn, x_kn, w_out, ffn_w_up, ffn_conv_w, ffn_conv_b, ffn_w_down):
    p = _prepare(norm_mix, norm_mem, norm_ffn, hg_w_in, hg_lb, hg_gn, gq_w_in, gq_qn, gq_kn,
                 gq_sink, x_w_kv, x_qn, x_kn, w_out, ffn_w_up, ffn_conv_w, ffn_conv_b, ffn_w_down)
    cos, sin_signed = _rope_tables(max(x_prompt.shape[1], x_sample.shape[1]))
    return (_trunk(x_prompt, mem_prompt, p, cos, sin_signed),
            _trunk(x_sample, mem_sample, p, cos, sin_signed))
```

```python
import functools

import jax
import jax.numpy as jnp
from jax import lax
from jax.experimental import pallas as pl
from jax.experimental.pallas import tpu as pltpu

F32 = jnp.float32
BF16 = jnp.bfloat16

D_MODEL = 1024
DEPTH = 2
MIX = 768
N_MEM = 256
X_HEADS = 4
X_HD = 64
X_W = X_HEADS * X_HD
HG_HEADS = 6
HG_DK = 128
HG_CHUNK = 64
N_Q = 12
N_KV = 4
HD = 64
GROUP = N_Q // N_KV
KV_W = N_KV * HD
WINDOW = 128
ROPE_THETA = 10000.0
D_FF = 2816
EPS = 1e-6

LANES = 128
MXU_N = 256
VMEM_LIMIT = 56 * 1024 * 1024

TM_HG = 1024
HG_SUB = 256
TM_PROJ = 1024
PROJ_SUB = 256
TM_ATT = 512
ATT_GROUPS = 3
TM_FFN = 1024
FFN_HALO = 8
FFN_TN = 256


def _params(n_axes):
    return pltpu.CompilerParams(
        dimension_semantics=("arbitrary",) * n_axes, vmem_limit_bytes=VMEM_LIMIT)


def _rms_rows(x, w):
    ms = jnp.mean(x * x, axis=-1, keepdims=True)
    return x * lax.rsqrt(ms + EPS) * w


def _head_ones(width, hd):
    r = lax.broadcasted_iota(jnp.int32, (width, width), 0) // hd
    c = lax.broadcasted_iota(jnp.int32, (width, width), 1) // hd
    return jnp.where(r == c, 1.0, 0.0).astype(BF16)


def _head_rms(x, hd):
    ones = _head_ones(MXU_N, hd)
    parts = []
    for g in range(x.shape[-1] // MXU_N):
        xs = x[:, g * MXU_N:(g + 1) * MXU_N]
        ss = jnp.dot((xs * xs).astype(BF16), ones, preferred_element_type=F32)
        parts.append(xs * lax.rsqrt(ss * (1.0 / hd) + EPS))
    return parts[0] if len(parts) == 1 else jnp.concatenate(parts, axis=-1)


def _dot_nt(a, b):
    return lax.dot_general(a, b, (((1,), (1,)), ((), ())), preferred_element_type=F32)


def _dot_tn(a, b):
    return lax.dot_general(a, b, (((0,), (0,)), ((), ())), preferred_element_type=F32)


def _round_robin(items, stages):
    inflight = [None] * len(stages)
    for tick in range(len(items) + len(stages) - 1):
        inflight = [(items[tick], None) if tick < len(items) else None] + inflight[:-1]
        for k, fn in enumerate(stages):
            if inflight[k] is not None:
                item, state = inflight[k]
                inflight[k] = (item, fn(item, state))


def _memkv_kernel(mem_ref, nw_ref, wkv_ref, kn_ref, mk_ref, mv_ref):
    mh = _rms_rows(mem_ref[0], nw_ref[0]).astype(BF16)
    kv = jnp.dot(mh, wkv_ref[0], preferred_element_type=F32)
    k = _head_rms(kv[:, :X_W], X_HD) * kn_ref[0]
    v = kv[:, X_W:]
    head = lax.broadcasted_iota(jnp.int32, (N_MEM, X_W), 1) // X_HD
    for h in range(X_HEADS):
        mk_ref[0, 0, h] = jnp.where(head == h, k, 0.0).astype(BF16)
        mv_ref[0, 0, h] = jnp.where(head == h, v, 0.0).astype(BF16)


def _memkv(mem, norm_mem, w_kv, kn_t):
    B = mem.shape[0]
    out = jax.ShapeDtypeStruct((DEPTH, B, X_HEADS, N_MEM, X_W), BF16)
    return pl.pallas_call(
        _memkv_kernel,
        grid=(DEPTH, B),
        in_specs=[
            pl.BlockSpec((1, N_MEM, D_MODEL), lambda l, b: (b, 0, 0)),
            pl.BlockSpec((1, 1, D_MODEL), lambda l, b: (l, 0, 0)),
            pl.BlockSpec((1, D_MODEL, 2 * X_W), lambda l, b: (l, 0, 0)),
            pl.BlockSpec((1, 1, X_W), lambda l, b: (l, 0, 0)),
        ],
        out_specs=[
            pl.BlockSpec((1, 1, X_HEADS, N_MEM, X_W), lambda l, b: (l, b, 0, 0, 0)),
            pl.BlockSpec((1, 1, X_HEADS, N_MEM, X_W), lambda l, b: (l, b, 0, 0, 0)),
        ],
        out_shape=[out, out],
        compiler_params=_params(2),
        name="mem_kv",
    )(mem, norm_mem, w_kv, kn_t)


def _mem_scores(xq, qn, mk_ref):
    q = ((_head_rms(xq, X_HD) * qn) * (X_HD ** -0.5)).astype(BF16)
    return [_dot_nt(q, mk_ref[h]) for h in range(X_HEADS)]


def _mem_out(scores, mv_ref):
    R = scores[0].shape[0]
    s = jnp.concatenate(scores, axis=0)
    p = jnp.exp(s - jnp.max(s, axis=-1, keepdims=True))
    pn = (p / jnp.sum(p, axis=-1, keepdims=True)).astype(BF16)
    acc = None
    for h in range(len(scores)):
        o = jnp.dot(pn[h * R:(h + 1) * R], mv_ref[h], preferred_element_type=F32)
        acc = o if acc is None else acc + o
    return acc


def _lower_bound(lb_ref, direction, layer):
    r = lb_ref[direction]
    e = jnp.exp(r - jnp.max(r, axis=0, keepdims=True))
    return jnp.sum(e[:layer + 1], axis=0, keepdims=True) / jnp.sum(e, axis=0, keepdims=True)


def _chunk_tri(n, reverse):
    r = lax.broadcasted_iota(jnp.int32, (n, n), 0)
    c = lax.broadcasted_iota(jnp.int32, (n, n), 1)
    same = (r // HG_CHUNK) == (c // HG_CHUNK)
    order = (c >= r) if reverse else (c <= r)
    return jnp.where(same & order, 1.0, 0.0).astype(BF16)


def _gla_prep(qs, f_raw, lb, qt_ref, kt_ref, reverse):
    R = qs.shape[0]
    f = lb + (1.0 - lb) * jax.nn.sigmoid(f_raw)
    g = jnp.log(f)
    k = 1.0 - f
    tri = _chunk_tri(R, reverse)
    g_hi = g.astype(BF16)
    g_lo = (g - g_hi.astype(F32)).astype(BF16)
    b = (jnp.dot(tri, g_hi, preferred_element_type=F32)
         + jnp.dot(tri, g_lo, preferred_element_type=F32))
    qt_ref[...] = (qs * jnp.exp(b)).astype(BF16)
    kt_ref[...] = (k * jnp.exp(-b)).astype(BF16)
    decays = []
    for c in range(R // HG_CHUNK):
        end_row = c * HG_CHUNK if reverse else (c + 1) * HG_CHUNK - 1
        decays.append(jnp.exp(b[end_row:end_row + 1, :]))
    return decays


def _gla_chunks(vb, decays, st_ref, qt_ref, kt_ref, o_ref, reverse):
    n_chunks = len(decays)
    ri = lax.broadcasted_iota(jnp.int32, (HG_CHUNK, HG_CHUNK), 0)
    ci = lax.broadcasted_iota(jnp.int32, (HG_CHUNK, HG_CHUNK), 1)
    keep = (ci >= ri) if reverse else (ci <= ri)
    order = list(range(n_chunks - 1, -1, -1) if reverse else range(n_chunks))

    def local(c):
        r0 = c * HG_CHUNK
        out = []
        for h in range(HG_HEADS):
            c0 = h * HG_DK
            qt = qt_ref[r0:r0 + HG_CHUNK, c0:c0 + HG_DK]
            kt = kt_ref[r0:r0 + HG_CHUNK, c0:c0 + HG_DK]
            vv = vb[r0:r0 + HG_CHUNK, c0:c0 + HG_DK]
            sc = jnp.where(keep, _dot_nt(qt, kt), 0.0).astype(BF16)
            out.append((qt, vv, sc, _dot_tn(vv, kt)))
        return out

    nxt = local(order[0])
    for n, c in enumerate(order):
        cur = nxt
        if n + 1 < n_chunks:
            nxt = local(order[n + 1])
        r0 = c * HG_CHUNK
        for h, (qt, vv, sc, kv) in enumerate(cur):
            c0 = h * HG_DK
            st = st_ref[c0:c0 + HG_DK, :]
            o_ref[r0:r0 + HG_CHUNK, c0:c0 + HG_DK] = (
                jnp.dot(sc, vv, preferred_element_type=F32) + _dot_nt(qt, st.astype(BF16)))
            st_ref[c0:c0 + HG_DK, :] = (st + kv) * decays[c][:, c0:c0 + HG_DK]


def _hg_fwd_kernel(x_ref, nw_ref, w_ref, lb_ref, qn_ref, mk_ref, mv_ref,
                   qs_ref, v_ref, fb_ref, sg_ref, of_ref, xo_ref,
                   st_ref, qt_ref, kt_ref):
    @pl.when(pl.program_id(1) == 0)
    def _():
        st_ref[...] = jnp.zeros_like(st_ref)

    lb = _lower_bound(lb_ref, 0, 0)
    rows = lambda i: slice(i * HG_SUB, (i + 1) * HG_SUB)

    def project(i, _):
        r = rows(i)
        h = _rms_rows(x_ref[0, r, :], nw_ref[...]).astype(BF16)
        proj = lambda c, w=MIX: jnp.dot(h, w_ref[:, c * MIX:c * MIX + w],
                                        preferred_element_type=F32)
        f_raw = proj(1)
        q = proj(0)
        cx = proj(5, X_W)
        vb = proj(3).astype(BF16)
        go = proj(4)
        fb_ref[0, r, :] = proj(2)
        qs = q * jax.nn.sigmoid(q)
        qs_ref[0, r, :] = qs.astype(BF16)
        v_ref[0, r, :] = vb
        sg_ref[0, r, :] = (go * jax.nn.sigmoid(go)).astype(BF16)
        return qs, f_raw, cx, vb

    def gates(i, state):
        qs, f_raw, cx, vb = state
        r = rows(i)
        mem_s = _mem_scores(cx, qn_ref[...], mk_ref.at[0, 0])
        decays = _gla_prep(qs, f_raw, lb, qt_ref.at[r], kt_ref.at[r], reverse=False)
        return mem_s, decays, vb

    def recur(i, state):
        mem_s, decays, vb = state
        r = rows(i)
        xo_ref[0, r, :] = _mem_out(mem_s, mv_ref.at[0, 0]).astype(BF16)
        _gla_chunks(vb, decays, st_ref, qt_ref.at[r], kt_ref.at[r], of_ref.at[0, r],
                    reverse=False)

    _round_robin(list(range(x_ref.shape[1] // HG_SUB)), [project, gates, recur])


def _hg_bwd_kernel(x_ref, qs_ref, v_ref, fb_ref, sg_ref, of_ref, xo_ref, lb_ref, gn_ref, wo_ref,
                   y_ref, st_ref, qt_ref, kt_ref, ob_ref):
    @pl.when(pl.program_id(1) == 0)
    def _():
        st_ref[...] = jnp.zeros_like(st_ref)

    lb = _lower_bound(lb_ref, 1, 0)
    rows = lambda i: slice(i * HG_SUB, (i + 1) * HG_SUB)

    def gates(i, _):
        r = rows(i)
        return _gla_prep(qs_ref[0, r, :].astype(F32), fb_ref[0, r, :], lb,
                         qt_ref.at[r], kt_ref.at[r], reverse=True)

    def recur(i, decays):
        r = rows(i)
        _gla_chunks(v_ref[0, r, :], decays, st_ref, qt_ref.at[r], kt_ref.at[r], ob_ref.at[r],
                    reverse=True)

    def output(i, _):
        r = rows(i)
        o = of_ref[0, r, :] + ob_ref[r, :]
        on = (_rms_rows(o, gn_ref[...]) * sg_ref[0, r, :].astype(F32)).astype(BF16)
        y_ref[0, r, :] = (x_ref[0, r, :]
                          + jnp.dot(on, wo_ref[:MIX, :], preferred_element_type=F32)
                          + jnp.dot(xo_ref[0, r, :], wo_ref[MIX:, :], preferred_element_type=F32))

    _round_robin(list(range(x_ref.shape[1] // HG_SUB - 1, -1, -1)), [gates, recur, output])


def _layer_spec(a, layer):
    return pl.BlockSpec((None,) + a.shape[1:], lambda b, j: (layer,) + (0,) * (a.ndim - 1),
                        pipeline_mode=pl.Buffered(1))


def _hgrn2_layer(x, layer, nw, w_in, hg_lb, gn, qn_t, mk, mv, w_out):
    B, S, _ = x.shape
    tm = TM_HG
    nb = S // tm
    tok = lambda w: pl.BlockSpec((1, tm, w), lambda b, j: (b, j, 0))
    full = lambda a: pl.BlockSpec(a.shape, lambda b, j: (0,) * a.ndim)
    memspec = pl.BlockSpec((1, 1, X_HEADS, N_MEM, X_W), lambda b, j: (0, b, 0, 0, 0))
    act = lambda w, dt: jax.ShapeDtypeStruct((B, S, w), dt)
    qs, v, fb, sg, of, xo = pl.pallas_call(
        _hg_fwd_kernel,
        grid=(B, nb),
        in_specs=[tok(D_MODEL), full(nw), full(w_in), full(hg_lb), full(qn_t), memspec, memspec],
        out_specs=[tok(MIX), tok(MIX), tok(MIX), tok(MIX), tok(MIX), tok(X_W)],
        out_shape=[act(MIX, BF16), act(MIX, BF16), act(MIX, F32), act(MIX, BF16),
                   act(MIX, F32), act(X_W, BF16)],
        scratch_shapes=[pltpu.VMEM((MIX, HG_DK), F32), pltpu.VMEM((tm, MIX), BF16),
                        pltpu.VMEM((tm, MIX), BF16)],
        compiler_params=_params(2),
        name="hgrn2_fwd",
    )(x, nw, w_in, hg_lb, qn_t, mk, mv)

    rtok = lambda w: pl.BlockSpec((1, tm, w), lambda b, j: (b, nb - 1 - j, 0))
    return pl.pallas_call(
        _hg_bwd_kernel,
        grid=(B, nb),
        in_specs=[rtok(D_MODEL), rtok(MIX), rtok(MIX), rtok(MIX), rtok(MIX), rtok(MIX), rtok(X_W),
                  full(hg_lb), full(gn), _layer_spec(w_out, layer)],
        out_specs=rtok(D_MODEL),
        out_shape=act(D_MODEL, F32),
        scratch_shapes=[pltpu.VMEM((MIX, HG_DK), F32), pltpu.VMEM((tm, MIX), BF16),
                        pltpu.VMEM((tm, MIX), BF16), pltpu.VMEM((tm, MIX), F32)],
        compiler_params=_params(2),
        name="hgrn2_bwd_out",
    )(x, qs, v, fb, sg, of, xo, hg_lb, gn, w_out)


def _ffn_kernel(x_ref, xp_ref, xn_ref, nw_ref, wu_ref, cw_ref, cb_ref, wd_ref, y_ref,
                h_ref, u_ref, a_ref):
    j = pl.program_id(1)
    nb = pl.num_programs(1)
    tm = x_ref.shape[1]
    rows = tm + 2 * FFN_HALO
    nw = nw_ref[...]
    x = x_ref[0]
    halo = jnp.concatenate([xn_ref[0] * jnp.where(j < nb - 1, 1.0, 0.0),
                            xp_ref[0] * jnp.where(j > 0, 1.0, 0.0)], axis=0)
    h_ref[0:tm, :] = _rms_rows(x, nw).astype(BF16)
    h_ref[tm:, :] = _rms_rows(halo, nw).astype(BF16)
    hh = h_ref[...]

    def up(t):
        for half in range(2):
            c0 = half * D_FF + t * FFN_TN
            u_ref[t % 2, half] = jnp.dot(hh, wu_ref[:, c0:c0 + FFN_TN], preferred_element_type=F32)

    def conv(t, half):
        c0 = half * D_FF + t * FFN_TN
        u = u_ref[t % 2, half]
        w = cw_ref[:, c0:c0 + FFN_TN]
        return (pltpu.roll(u, 1, 0)[0:tm] * w[0:1] + u[0:tm] * w[1:2]
                + pltpu.roll(u, rows - 1, 0)[0:tm] * w[2:3] + cb_ref[:, c0:c0 + FFN_TN])

    n_tiles = D_FF // FFN_TN
    up(0)
    for t in range(n_tiles):
        if t + 1 < n_tiles:
            up(t + 1)
        gate = conv(t, 0)
        val = conv(t, 1)
        a_ref[:, t * FFN_TN:(t + 1) * FFN_TN] = (gate * jax.nn.sigmoid(gate) * val).astype(BF16)
    y_ref[0] = x + jnp.dot(a_ref[...], wd_ref[...], preferred_element_type=F32)


def _ffn(x, layer, nw, w_up, conv_w, conv_b, w_down):
    B, S, _ = x.shape
    tm = TM_FFN
    nb = S // tm
    r = tm // FFN_HALO
    last = S // FFN_HALO - 1
    full = lambda a: _layer_spec(a, layer)
    return pl.pallas_call(
        _ffn_kernel,
        grid=(B, nb),
        in_specs=[
            pl.BlockSpec((1, tm, D_MODEL), lambda b, j: (b, j, 0)),
            pl.BlockSpec((1, FFN_HALO, D_MODEL), lambda b, j: (b, jnp.maximum(j * r - 1, 0), 0)),
            pl.BlockSpec((1, FFN_HALO, D_MODEL),
                         lambda b, j: (b, jnp.minimum((j + 1) * r, last), 0)),
            full(nw), full(w_up), full(conv_w), full(conv_b), full(w_down),
        ],
        out_specs=pl.BlockSpec((1, tm, D_MODEL), lambda b, j: (b, j, 0)),
        out_shape=jax.ShapeDtypeStruct((B, S, D_MODEL), F32),
        scratch_shapes=[pltpu.VMEM((tm + 2 * FFN_HALO, D_MODEL), BF16),
                        pltpu.VMEM((2, 2, tm + 2 * FFN_HALO, FFN_TN), F32),
                        pltpu.VMEM((tm, D_FF), BF16)],
        compiler_params=_params(2),
        name="conv_glu",
    )(x, x, x, nw, w_up, conv_w, conv_b, w_down)


def _rope(x, cos, sin_signed, first_half):
    rot = jnp.where(first_half, pltpu.roll(x, LANES - HD // 2, 1), pltpu.roll(x, HD // 2, 1))
    return x * cos + rot * sin_signed


def _gq_proj_kernel(x_ref, nw_ref, w_ref, qn_ref, kn_ref, cos_ref, sin_ref, xqn_ref,
                    mk_ref, mv_ref, q_ref, k_ref, v_ref, xo_ref):
    rows = lambda i: slice(i * PROJ_SUB, (i + 1) * PROJ_SUB)

    def project(i, _):
        h = _rms_rows(x_ref[0, rows(i), :], nw_ref[...]).astype(BF16)
        return jnp.dot(h, w_ref[...], preferred_element_type=F32)

    def rotate(i, cols):
        r = rows(i)
        cos = cos_ref[r, :]
        sin = sin_ref[r, :]
        first_half = (lax.broadcasted_iota(jnp.int32, cos.shape, 1) % HD) < HD // 2
        qn = _head_rms(cols[:, :MIX], HD) * qn_ref[...]
        for g in range(MIX // LANES):
            sl = slice(g * LANES, (g + 1) * LANES)
            q_ref[0, r, sl] = (_rope(qn[:, sl], cos, sin, first_half) * (HD ** -0.5)).astype(BF16)
        kn = _head_rms(cols[:, MIX:MIX + KV_W], HD) * kn_ref[...]
        v = cols[:, MIX + KV_W:MIX + 2 * KV_W]
        for g in range(KV_W // LANES):
            sl = slice(g * LANES, (g + 1) * LANES)
            sw = slice(KV_W + g * LANES, KV_W + (g + 1) * LANES)
            kr = _rope(kn[:, sl], cos, sin, first_half)
            k_ref[0, r, sl] = kr.astype(BF16)
            k_ref[0, r, sw] = pltpu.roll(kr, HD, 1).astype(BF16)
            v_ref[0, r, sl] = v[:, sl].astype(BF16)
            v_ref[0, r, sw] = pltpu.roll(v[:, sl], HD, 1).astype(BF16)
        return _mem_scores(cols[:, MIX + 2 * KV_W:], xqn_ref[...], mk_ref.at[0, 0])

    def memory(i, mem_s):
        xo_ref[0, rows(i), :] = _mem_out(mem_s, mv_ref.at[0, 0]).astype(BF16)

    _round_robin(list(range(x_ref.shape[1] // PROJ_SUB)), [project, rotate, memory])


def _win_attn_kernel(sink_ref, x_ref, q_ref, k_ref, kp_ref, kn_ref, v_ref, vp_ref, vn_ref,
                     xo_ref, wo_ref, y_ref, kv_ref, vv_ref, bias_ref, o_ref, *, seq_len):
    j = pl.program_id(1)
    tm = x_ref.shape[1]
    rows = tm + 2 * WINDOW
    half = lax.broadcasted_iota(jnp.int32, (rows, LANES), 1) // HD
    for src, prev, nxt, dst in ((k_ref, kp_ref, kn_ref, kv_ref), (v_ref, vp_ref, vn_ref, vv_ref)):
        for kvh in range(N_KV):
            for hq in range(2):
                c0 = (kvh // 2) * LANES + (0 if kvh % 2 == hq else KV_W)
                cat = jnp.concatenate([prev[0, :, c0:c0 + LANES], src[0, :, c0:c0 + LANES],
                                       nxt[0, :, c0:c0 + LANES]], axis=0)
                dst[kvh * 2 + hq] = jnp.where(half == hq, cat, jnp.zeros_like(cat))

    qi = lax.broadcasted_iota(jnp.int32, (WINDOW, 3 * WINDOW), 0)
    kj = lax.broadcasted_iota(jnp.int32, (WINDOW, 3 * WINDOW), 1)
    n_sub = tm // WINDOW
    for i in range(n_sub):
        kpos = j * tm + (i - 1) * WINDOW + kj
        ok = (kj - qi >= 0) & (kj - qi <= 2 * WINDOW) & (kpos >= 0) & (kpos < seq_len)
        bias_ref[i] = jnp.where(ok, 0.0, -jnp.inf)

    slab = lambda g, hq: ((2 * g + hq) // GROUP) * 2 + hq

    def scores(i, gs):
        r0 = i * WINDOW
        out = []
        for g in gs:
            qp = q_ref[0, r0:r0 + WINDOW, g * LANES:(g + 1) * LANES]
            out += [_dot_nt(qp, kv_ref[slab(g, hq), r0:r0 + 3 * WINDOW, :]) for hq in range(2)]
        return out

    groups = [tuple(range(g0, g0 + ATT_GROUPS)) for g0 in range(0, N_Q // 2, ATT_GROUPS)]
    work = [(i, gs) for i in range(n_sub) for gs in groups]
    heads = lambda gs: [(g, hq) for g in gs for hq in range(2)]

    def stage_max(item, s_raw):
        i, gs = item
        ms = [jnp.maximum(jnp.max(s + bias_ref[i], axis=-1, keepdims=True), sink_ref[2 * g + hq])
              for s, (g, hq) in zip(s_raw, heads(gs))]
        return s_raw, ms

    def stage_exp(item, st):
        i, gs = item
        ps, dens = [], []
        for s, m, (g, hq) in zip(*st, heads(gs)):
            p = jnp.exp(s + bias_ref[i] - m)
            dens.append(jnp.sum(p, axis=-1, keepdims=True) + jnp.exp(sink_ref[2 * g + hq] - m))
            ps.append(p.astype(BF16))
        return ps, dens

    def stage_out(item, st):
        i, gs = item
        r0 = i * WINDOW
        ps, dens = st
        for n, g in enumerate(gs):
            acc = None
            for hq in range(2):
                o = jnp.dot(ps[2 * n + hq], vv_ref[slab(g, hq), r0:r0 + 3 * WINDOW, :],
                            preferred_element_type=F32) / dens[2 * n + hq]
                acc = o if acc is None else acc + o
            o_ref[r0:r0 + WINDOW, g * LANES:(g + 1) * LANES] = acc.astype(BF16)

    _round_robin(work, [lambda item, _: scores(*item), stage_max, stage_exp, stage_out])
    y_ref[0] = (x_ref[0]
                + jnp.dot(o_ref[...], wo_ref[:MIX, :], preferred_element_type=F32)
                + jnp.dot(xo_ref[0], wo_ref[MIX:, :], preferred_element_type=F32))


def _gqa_layer(x, layer, nw, w_in, qn_t, kn_t, sink, cos, sin_signed, xqn_t, mk, mv, w_out):
    q, k2, v2, xo = _gqa_proj(x, nw, w_in, qn_t, kn_t, cos, sin_signed, xqn_t, mk, mv)
    return _win_attn(x, q, k2, v2, xo, sink, layer, w_out)


def _gqa_proj(x, nw, w_in, qn_t, kn_t, cos, sin_signed, xqn_t, mk, mv):
    B, S, _ = x.shape
    tm = TM_PROJ
    tok = lambda w: pl.BlockSpec((1, tm, w), lambda b, j: (b, j, 0))
    full = lambda a: pl.BlockSpec(a.shape, lambda b, j: (0,) * a.ndim)
    memspec = pl.BlockSpec((1, 1, X_HEADS, N_MEM, X_W), lambda b, j: (1, b, 0, 0, 0))
    act = lambda w, dt: jax.ShapeDtypeStruct((B, S, w), dt)
    return pl.pallas_call(
        _gq_proj_kernel,
        grid=(B, S // tm),
        in_specs=[tok(D_MODEL), full(nw), full(w_in), full(qn_t), full(kn_t),
                  pl.BlockSpec((tm, LANES), lambda b, j: (j, 0)),
                  pl.BlockSpec((tm, LANES), lambda b, j: (j, 0)),
                  full(xqn_t), memspec, memspec],
        out_specs=[tok(MIX), tok(2 * KV_W), tok(2 * KV_W), tok(X_W)],
        out_shape=[act(MIX, BF16), act(2 * KV_W, BF16), act(2 * KV_W, BF16), act(X_W, BF16)],
        compiler_params=_params(2),
        name="gqa_proj",
    )(x, nw, w_in, qn_t, kn_t, cos, sin_signed, xqn_t, mk, mv)


def _win_attn(x, q, k2, v2, xo, sink, layer, w_out):
    B, S, _ = x.shape
    ta = TM_ATT
    tok = lambda w, t: pl.BlockSpec((1, t, w), lambda b, j: (b, j, 0))
    full = lambda a: _layer_spec(a, layer)
    act = lambda w, dt: jax.ShapeDtypeStruct((B, S, w), dt)
    r = ta // WINDOW
    last = S // WINDOW - 1
    prev = pl.BlockSpec((1, WINDOW, 2 * KV_W), lambda b, j: (b, jnp.maximum(j * r - 1, 0), 0))
    nxt = pl.BlockSpec((1, WINDOW, 2 * KV_W), lambda b, j: (b, jnp.minimum((j + 1) * r, last), 0))
    return pl.pallas_call(
        functools.partial(_win_attn_kernel, seq_len=S),
        grid=(B, S // ta),
        in_specs=[pl.BlockSpec(memory_space=pltpu.SMEM),
                  tok(D_MODEL, ta), tok(MIX, ta),
                  tok(2 * KV_W, ta), prev, nxt, tok(2 * KV_W, ta), prev, nxt,
                  tok(X_W, ta), full(w_out)],
        out_specs=tok(D_MODEL, ta),
        out_shape=act(D_MODEL, F32),
        scratch_shapes=[pltpu.VMEM((2 * N_KV, ta + 2 * WINDOW, LANES), BF16),
                        pltpu.VMEM((2 * N_KV, ta + 2 * WINDOW, LANES), BF16),
                        pltpu.VMEM((ta // WINDOW, WINDOW, 3 * WINDOW), F32),
                        pltpu.VMEM((ta, MIX), BF16)],
        compiler_params=_params(2),
        name="win_attn_out",
    )(sink, x, q, k2, k2, k2, v2, v2, v2, xo, w_out)


def _rope_tables(seq_len):
    freqs = ROPE_THETA ** (-jnp.arange(0, HD, 2, dtype=F32) / HD)
    ang = jnp.arange(seq_len, dtype=F32)[:, None] * freqs[None, :]
    cos, sin = jnp.cos(ang), jnp.sin(ang)
    return (jnp.tile(cos, (1, 2 * LANES // HD)),
            jnp.tile(jnp.concatenate([-sin, sin], axis=1), (1, LANES // HD)))


def _trunk(x, mem, p, cos, sin_signed):
    mk, mv = _memkv(mem, p["norm_mem"], p["x_w_kv"], p["x_kn"])
    for i in range(DEPTH):
        if i % 2 == 0:
            a = i // 2
            assert a == 0 and i == 0, "the HGRN2 kernels read layer 0's forget-gate lower bounds"
            x = _hgrn2_layer(x, i, p["norm_mix"][i], p["hg_w_in"][a], p["hg_lb"], p["hg_gn"][a],
                             p["x_qn"][i], mk, mv, p["w_out"])
        else:
            b = i // 2
            assert i == 1, "the attention kernels read layer 1's memory keys and values"
            x = _gqa_layer(x, i, p["norm_mix"][i], p["gq_w_in"][b], p["gq_qn"][b], p["gq_kn"][b],
                           p["gq_sink"][b], cos, sin_signed, p["x_qn"][i], mk, mv, p["w_out"])
        x = _ffn(x, i, p["norm_ffn"], p["ffn_w_up"], p["ffn_conv_w"], p["ffn_conv_b"],
                 p["ffn_w_down"])
    return x


def _prepare(norm_mix, norm_mem, norm_ffn, hg_w_in, hg_lb, hg_gn, gq_w_in, gq_qn, gq_kn, gq_sink,
             x_w_kv, x_qn, x_kn, w_out, ffn_w_up, ffn_conv_w, ffn_conv_b, ffn_w_down):
    row = lambda a: a.reshape(a.shape[0], 1, a.shape[-1])
    return {
        "norm_mix": row(norm_mix), "norm_mem": row(norm_mem), "norm_ffn": row(norm_ffn),
        "hg_w_in": hg_w_in.astype(BF16), "hg_lb": hg_lb, "hg_gn": row(hg_gn),
        "gq_w_in": gq_w_in.astype(BF16),
        "gq_qn": row(jnp.tile(gq_qn, (1, N_Q))), "gq_kn": row(jnp.tile(gq_kn, (1, N_KV))),
        "gq_sink": gq_sink,
        "x_w_kv": x_w_kv.astype(BF16),
        "x_qn": row(jnp.tile(x_qn, (1, X_HEADS))), "x_kn": row(jnp.tile(x_kn, (1, X_HEADS))),
        "w_out": w_out.astype(BF16),
        "ffn_w_up": ffn_w_up.astype(BF16), "ffn_conv_w": ffn_conv_w, "ffn_conv_b": row(ffn_conv_b),
        "ffn_w_down": ffn_w_down.astype(BF16),
    }


def kernel(x_prompt, x_sample, mem_prompt, mem_sample, norm_mix, norm_mem, norm_ffn, hg_w_in, hg_lb, hg_gn, gq_w_in, gq_qn, gq_kn, gq_sink, x_w_kv, x_qn, x_kn, w_out, ffn_w_up, ffn_conv_w, ffn_conv_b, ffn_w_down):
    p = _prepare(norm_mix, norm_mem, norm_ffn, hg_w_in, hg_lb, hg_gn, gq_w_in, gq_qn, gq_kn,
                 gq_sink, x_w_kv, x_qn, x_kn, w_out, ffn_w_up, ffn_conv_w, ffn_conv_b, ffn_w_down)
    cos, sin_signed = _rope_tables(max(x_prompt.shape[1], x_sample.shape[1]))
    return (_trunk(x_prompt, mem_prompt, p, cos, sin_signed),
            _trunk(x_sample, mem_sample, p, cos, sin_signed))
```
